```python
import math
import jax, jax.numpy as jnp
from jax import lax
import numpy as np

D_MODEL = 2048
BATCH = 1
SEQ = 8192
DEPTH = 4

N_A = DEPTH // 2
N_B = DEPTH - N_A
EPS = 1e-6
D_FF = 4 * D_MODEL

GLA_HEADS = 4
GLA_DK = (D_MODEL // 2) // GLA_HEADS
GLA_DV = D_MODEL // GLA_HEADS
GLA_GATE_RANK = 16
GLA_GATE_TAU = 16.0
GLA_CHUNK = 64
GLA_QK = GLA_HEADS * GLA_DK
GLA_VV = GLA_HEADS * GLA_DV
GLA_IN = 2 * GLA_QK + 2 * GLA_VV + GLA_GATE_RANK

SWA_HEAD_DIM = 64
SWA_Q_HEADS = D_MODEL // SWA_HEAD_DIM
SWA_KV_HEADS = SWA_Q_HEADS // 8
SWA_GROUP = SWA_Q_HEADS // SWA_KV_HEADS
SWA_WINDOW = 128
SWA_BLOCK = 128

kernel_name = "yoco_gla_swa_sink_hybrid"


def rmsnorm(x, g):
    xf = x.astype(jnp.float32)
    y = xf * lax.rsqrt(jnp.mean(xf * xf, axis=-1, keepdims=True) + EPS)
    return (y * g.astype(jnp.float32)).astype(x.dtype)


def sqrelu_mlp(h, w1, w2):
    u = jax.nn.relu(h @ w1)
    return (u * u) @ w2


def gla_mixer(h, w_in, w_g2, b_g, g_o, w_o):
    B, S, _ = h.shape
    H, dk, dv, C = GLA_HEADS, GLA_DK, GLA_DV, GLA_CHUNK
    nC = S // C
    f32 = jnp.float32
    proj = h @ w_in
    q, k, v, r, glr = jnp.split(
        proj, [GLA_QK, 2 * GLA_QK, 2 * GLA_QK + GLA_VV, 2 * GLA_QK + 2 * GLA_VV], axis=-1)
    log_a = jax.nn.log_sigmoid((glr @ w_g2 + b_g).astype(f32)) / GLA_GATE_TAU

    def to_chunks(t, d):
        return t.reshape(B, nC, C, H, d).transpose(1, 0, 3, 2, 4)

    qc = to_chunks(q.astype(f32) * (dk ** -0.5), dk)
    kc = to_chunks(k.astype(f32), dk)
    vc = to_chunks(v.astype(f32), dv)
    bc = jnp.cumsum(to_chunks(log_a, dk), axis=-2)
    causal = jnp.tril(jnp.ones((C, C), dtype=bool))[:, :, None]

    def step(state, inp):
        qi, ki, vi, bi = inp
        o_inter = jnp.einsum('bhcd,bhde->bhce', qi * jnp.exp(bi), state)
        diff = bi[:, :, :, None, :] - bi[:, :, None, :, :]
        decay = jnp.exp(jnp.where(causal, diff, -jnp.inf))
        attn = jnp.einsum('bhid,bhjd,bhijd->bhij', qi, ki, decay)
        o = o_inter + jnp.einsum('bhij,bhje->bhie', attn, vi)
        b_last = bi[:, :, -1:, :]
        k_dec = ki * jnp.exp(b_last - bi)
        state = jnp.exp(b_last[:, :, 0, :])[..., None] * state + \
            jnp.einsum('bhcd,bhce->bhde', k_dec, vi)
        return state, o

    state0 = jnp.zeros((B, H, dk, dv), f32)
    _, oc = lax.scan(step, state0, (qc, kc, vc, bc))
    o = oc.transpose(1, 0, 3, 2, 4).reshape(B, S, H, dv)
    o = rmsnorm(o, g_o)
    o = o * jax.nn.silu(r.astype(f32)).reshape(B, S, H, dv)
    return (o.reshape(B, S, H * dv) @ w_o.astype(f32)).astype(h.dtype)


def shared_kv(h, g_kv, w_k, w_v, g_k):
    B, S, _ = h.shape
    nB = S // SWA_BLOCK
    u = rmsnorm(h, g_kv)
    k = rmsnorm((u @ w_k).reshape(B, S, SWA_KV_HEADS, SWA_HEAD_DIM), g_k)
    v = (u @ w_v).reshape(B, S, SWA_KV_HEADS, SWA_HEAD_DIM)

    def band(t):
        tb = t.reshape(B, nB, SWA_BLOCK, SWA_KV_HEADS, SWA_HEAD_DIM)
        prev = jnp.pad(tb, ((0, 0), (1, 0), (0, 0), (0, 0), (0, 0)))[:, :-1]
        return jnp.concatenate([prev, tb], axis=2)

    return band(k), band(v)


def swa_sink_mixer(h, k_band, v_band, w_q, g_q, sinks, w_o):
    B, S, _ = h.shape
    nB = S // SWA_BLOCK
    q = (h @ w_q).reshape(B, S, SWA_KV_HEADS, SWA_GROUP, SWA_HEAD_DIM)
    q = rmsnorm(q, g_q) * (SWA_HEAD_DIM ** -0.5)
    qb = q.reshape(B, nB, SWA_BLOCK, SWA_KV_HEADS, SWA_GROUP, SWA_HEAD_DIM)
    s = jnp.einsum('bnqkgd,bnjkd->bkgnqj', qb, k_band).astype(jnp.float32)
    qi = jnp.arange(SWA_BLOCK)[:, None]
    kj = jnp.arange(2 * SWA_BLOCK)[None, :]
    rel = qi + SWA_BLOCK - kj
    valid = (rel >= 0) & (rel < SWA_WINDOW)
    first = (jnp.arange(nB)[:, None, None] > 0) | (kj >= SWA_BLOCK)[None]
    mask = valid[None] & first
    s = jnp.where(mask, s, -jnp.inf)
    sink = jnp.broadcast_to(
        sinks.astype(jnp.float32).reshape(1, SWA_KV_HEADS, SWA_GROUP, 1, 1, 1), s.shape[:-1] + (1,))
    p = jax.nn.softmax(jnp.concatenate([s, sink], axis=-1), axis=-1)[..., :-1]
    o = jnp.einsum('bkgnqj,bnjkd->bnqkgd', p.astype(v_band.dtype), v_band)
    return o.reshape(B, S, D_MODEL) @ w_o


def setup_inputs(seed: int = 0) -> dict:
    key = jax.random.key(seed)
    ks = jax.random.split(key, 20)
    n = jax.random.normal
    f = jnp.float32
    D = D_MODEL
    kvw = SWA_KV_HEADS * SWA_HEAD_DIM
    return {
        "x": n(ks[0], (BATCH, SEQ, D), f),
        "norm_mix": 1.0 + 0.02 * n(ks[1], (DEPTH, D), f),
        "norm_mlp": 1.0 + 0.02 * n(ks[2], (DEPTH, D), f),
        "mlp_w1": n(ks[3], (DEPTH, D, D_FF), f) * D ** -0.5,
        "mlp_w2": n(ks[4], (DEPTH, D_FF, D), f) * D_FF ** -0.5,
        "a_w_in": n(ks[5], (N_A, D, GLA_IN), f) * D ** -0.5,
        "a_w_g2": n(ks[6], (N_A, GLA_GATE_RANK, GLA_QK), f) * GLA_GATE_RANK ** -0.5,
        "a_b_g": 0.1 * n(ks[7], (N_A, GLA_QK), f),
        "a_g_o": 1.0 + 0.02 * n(ks[8], (N_A, GLA_DV), f),
        "a_w_o": n(ks[9], (N_A, GLA_VV, D), f) * GLA_VV ** -0.5,
        "kv_norm": 1.0 + 0.02 * n(ks[10], (D,), f),
        "kv_w_k": n(ks[11], (D, kvw), f) * D ** -0.5,
        "kv_w_v": n(ks[12], (D, kvw), f) * D ** -0.5,
        "kv_g_k": 1.0 + 0.02 * n(ks[13], (SWA_HEAD_DIM,), f),
        "b_w_q": n(ks[14], (N_B, D, SWA_Q_HEADS * SWA_HEAD_DIM), f) * D ** -0.5,
        "b_g_q": 1.0 + 0.02 * n(ks[15], (N_B, SWA_HEAD_DIM), f),
        "b_sinks": 0.5 * n(ks[16], (N_B, SWA_Q_HEADS), f),
        "b_w_o": n(ks[17], (N_B, SWA_Q_HEADS * SWA_HEAD_DIM, D), f) * D ** -0.5,
    }


def reference(x, norm_mix, norm_mlp, mlp_w1, mlp_w2, a_w_in, a_w_g2, a_b_g, a_g_o, a_w_o,
              kv_norm, kv_w_k, kv_w_v, kv_g_k, b_w_q, b_g_q, b_sinks, b_w_o):
    h = x
    k_band = None
    v_band = None
    for layer in range(DEPTH):
        u = rmsnorm(h, norm_mix[layer])
        if layer < N_A:
            i = layer
            h = h + gla_mixer(u, a_w_in[i], a_w_g2[i], a_b_g[i], a_g_o[i], a_w_o[i])
        else:
            if layer == N_A:
                k_band, v_band = shared_kv(h, kv_norm, kv_w_k, kv_w_v, kv_g_k)
            j = layer - N_A
            h = h + swa_sink_mixer(u, k_band, v_band, b_w_q[j], b_g_q[j], b_sinks[j], b_w_o[j])
        h = h + sqrelu_mlp(rmsnorm(h, norm_mlp[layer]), mlp_w1[layer], mlp_w2[layer])
    return h
```

```python
import functools

import jax
import jax.numpy as jnp
import numpy as np
from jax import lax
from jax.experimental import pallas as pl
from jax.experimental.pallas import tpu as pltpu

F32 = jnp.float32
BF16 = jnp.bfloat16

EPS = 1e-6
D_MODEL = 2048
SEQ = 8192
DEPTH = 4
N_GLA = DEPTH // 2
D_FF = 4 * D_MODEL

GLA_HEADS = 4
GLA_DK = 256
GLA_DV = 512
GLA_RANK = 16
GLA_TAU = 16.0
GLA_QK = GLA_HEADS * GLA_DK
GLA_VV = GLA_HEADS * GLA_DV
GLA_MAIN = 2 * GLA_QK + 2 * GLA_VV
GLA_CHUNK = 256
LANES = 128

SWA_HD = 64
SWA_QH = 32
SWA_KVH = 4
SWA_GROUP = SWA_QH // SWA_KVH
SWA_WINDOW = 128
SWA_BLOCK = 128
SWA_KVW = SWA_KVH * SWA_HD

VMEM_LIMIT = 56 * 1024 * 1024


def _params(n_axes):
    return pltpu.CompilerParams(
        dimension_semantics=("arbitrary",) * n_axes, vmem_limit_bytes=VMEM_LIMIT)


def _rmsnorm_rows(x, g):
    ms = jnp.mean(x * x, axis=-1, keepdims=True)
    return x * lax.rsqrt(ms + EPS) * g


def _normalize_into(x_ref, g_ref, hn_ref, rows_per_step=256):
    n = x_ref.shape[0] // rows_per_step

    def body(r, carry):
        sl = pl.ds(pl.multiple_of(r * rows_per_step, rows_per_step), rows_per_step)
        hn_ref[sl, :] = _rmsnorm_rows(x_ref[sl, :], g_ref[...]).astype(BF16)
        return carry

    lax.fori_loop(0, n, body, 0)


def _norm_matmul_kernel(x_ref, g_ref, w_ref, o_ref, hn_ref):
    @pl.when(pl.program_id(1) == 0)
    def _():
        _normalize_into(x_ref, g_ref, hn_ref)

    o_ref[...] = jnp.dot(hn_ref[...], w_ref[...], preferred_element_type=F32).astype(o_ref.dtype)


def _norm_matmul(x, g, w, out_dtype, tm, tn):
    s, d = x.shape
    n = w.shape[1]
    return pl.pallas_call(
        _norm_matmul_kernel,
        grid=(s // tm, n // tn),
        in_specs=[
            pl.BlockSpec((tm, d), lambda i, j: (i, 0)),
            pl.BlockSpec((1, d), lambda i, j: (0, 0)),
            pl.BlockSpec((d, tn), lambda i, j: (0, j)),
        ],
        out_specs=pl.BlockSpec((tm, tn), lambda i, j: (i, j)),
        out_shape=jax.ShapeDtypeStruct((s, n), out_dtype),
        scratch_shapes=[pltpu.VMEM((tm, d), BF16)],
        compiler_params=_params(2),
        name="norm_matmul",
    )(x, g.reshape(1, d), w)


def _matmul_residual_kernel(a_ref, w_ref, r_ref, o_ref):
    o_ref[...] = r_ref[...] + jnp.dot(a_ref[...], w_ref[...], preferred_element_type=F32)


def _matmul_residual(a, w, res, tm, tn):
    s, k = a.shape
    n = w.shape[1]
    return pl.pallas_call(
        _matmul_residual_kernel,
        grid=(s // tm, n // tn),
        in_specs=[
            pl.BlockSpec((tm, k), lambda i, j: (i, 0)),
            pl.BlockSpec((k, tn), lambda i, j: (0, j)),
            pl.BlockSpec((tm, tn), lambda i, j: (i, j)),
        ],
        out_specs=pl.BlockSpec((tm, tn), lambda i, j: (i, j)),
        out_shape=jax.ShapeDtypeStruct((s, n), F32),
        compiler_params=_params(2),
        name="matmul_residual",
    )(a, w, res)


def _mlp_kernel(x_ref, g_ref, w1_ref, w2_ref, o_ref, hn_ref):
    @pl.when(pl.program_id(1) == 0)
    def _():
        _normalize_into(x_ref, g_ref, hn_ref)
        o_ref[...] = x_ref[...]

    u = jnp.dot(hn_ref[...], w1_ref[...], preferred_element_type=F32)
    u = jnp.maximum(u, 0.0)
    u = (u * u).astype(BF16)
    o_ref[...] += jnp.dot(u, w2_ref[...], preferred_element_type=F32)


def _mlp(x, g, w1, w2, tm, tf):
    s, d = x.shape
    f = w1.shape[1]
    return pl.pallas_call(
        _mlp_kernel,
        grid=(s // tm, f // tf),
        in_specs=[
            pl.BlockSpec((tm, d), lambda i, j: (i, 0)),
            pl.BlockSpec((1, d), lambda i, j: (0, 0)),
            pl.BlockSpec((d, tf), lambda i, j: (0, j)),
            pl.BlockSpec((tf, d), lambda i, j: (j, 0)),
        ],
        out_specs=pl.BlockSpec((tm, d), lambda i, j: (i, 0)),
        out_shape=jax.ShapeDtypeStruct((s, d), F32),
        scratch_shapes=[pltpu.VMEM((tm, d), BF16)],
        compiler_params=_params(2),
        name="mlp",
    )(x, g.reshape(1, d), w1, w2)


@functools.lru_cache(maxsize=None)
def _gla_constants(c):
    i = np.arange(c)[:, None]
    t = np.arange(c)[None, :]
    mats = [t <= i, t > i]
    masks = []
    s = c // 2
    while s >= 1:
        pos = i % (2 * s)
        ref_row = i - pos + s
        upper = pos >= s
        mats.append(np.where(upper, (t > ref_row) & (t <= i), (t > i) & (t <= ref_row)))
        same_block = (i // (2 * s)) == (t // (2 * s))
        masks.append(same_block & upper & ((t % (2 * s)) < s))
        s //= 2
    masks.append(i == t)
    sums = np.concatenate(mats, axis=0).astype(np.float32)
    return sums, np.stack(masks).astype(np.float32)


def _dot_nt(a, b):
    return lax.dot_general(a, b, (((1,), (1,)), ((), ())), preferred_element_type=F32)


def _dot_tn(a, b):
    return lax.dot_general(a, b, (((0,), (0,)), ((), ())), preferred_element_type=F32)


def _gla_kernel(q_ref, k_ref, v_ref, r_ref, glr_ref, wg_ref, bg_ref, go_ref, sums_ref, mask_ref,
                o_ref, st_ref):
    c = GLA_CHUNK
    n_levels = mask_ref.shape[0] - 1

    @pl.when(pl.program_id(1) == 0)
    def _():
        st_ref[...] = jnp.zeros_like(st_ref)

    gate_in = jnp.dot(glr_ref[...].astype(BF16), wg_ref[...], preferred_element_type=F32)
    gate_in = gate_in + bg_ref[...]
    log_sig = jnp.minimum(gate_in, 0.0) - jnp.log1p(jnp.exp(-jnp.abs(gate_in)))
    log_a = log_sig / GLA_TAU

    la_hi = log_a.astype(BF16)
    la_lo = (log_a - la_hi.astype(F32)).astype(BF16)
    sums = sums_ref[...]
    part = (jnp.dot(sums, la_hi, preferred_element_type=F32)
            + jnp.dot(sums, la_lo, preferred_element_type=F32))

    q = q_ref[...].astype(F32) * (GLA_DK ** -0.5)
    k = k_ref[...].astype(F32)
    v = v_ref[...]

    b = part[0:c]
    state_t = st_ref[...]
    o = _dot_nt((q * jnp.exp(b)).astype(BF16), state_t.astype(BF16))

    attn = _dot_nt(q.astype(BF16), k.astype(BF16)) * mask_ref[n_levels]
    for lvl in range(n_levels):
        e = jnp.exp(part[(2 + lvl) * c:(3 + lvl) * c])
        attn = attn + _dot_nt((q * e).astype(BF16), (k * e).astype(BF16)) * mask_ref[lvl]
    o = o + jnp.dot(attn.astype(BF16), v, preferred_element_type=F32)

    k_dec = (k * jnp.exp(part[c:2 * c])).astype(BF16)
    st_ref[...] = state_t * jnp.exp(b[c - 1:c, :]) + _dot_tn(v, k_dec)

    o = _rmsnorm_rows(o, go_ref[...])
    r = r_ref[...].astype(F32)
    o_ref[...] = (o * (r * jax.nn.sigmoid(r))).astype(o_ref.dtype)


def _gla_core(proj, glr, w_g2, b_g, g_o):
    s = proj.shape[0]
    c = GLA_CHUNK
    sums, masks = _gla_constants(c)
    sums = jnp.asarray(sums, dtype=BF16)
    masks = jnp.asarray(masks, dtype=F32)
    w_g2p = jnp.pad(w_g2, ((0, LANES - GLA_RANK), (0, 0))).astype(BF16)
    k_off = GLA_QK // GLA_DK
    v_off = 2 * GLA_QK // GLA_DV
    r_off = (2 * GLA_QK + GLA_VV) // GLA_DV
    return pl.pallas_call(
        _gla_kernel,
        grid=(GLA_HEADS, s // c),
        in_specs=[
            pl.BlockSpec((c, GLA_DK), lambda h, i: (i, h)),
            pl.BlockSpec((c, GLA_DK), lambda h, i: (i, k_off + h)),
            pl.BlockSpec((c, GLA_DV), lambda h, i: (i, v_off + h)),
            pl.BlockSpec((c, GLA_DV), lambda h, i: (i, r_off + h)),
            pl.BlockSpec((c, LANES), lambda h, i: (i, 0)),
            pl.BlockSpec((LANES, GLA_DK), lambda h, i: (0, h)),
            pl.BlockSpec((1, GLA_DK), lambda h, i: (0, h)),
            pl.BlockSpec((1, GLA_DV), lambda h, i: (0, 0)),
            pl.BlockSpec(sums.shape, lambda h, i: (0, 0)),
            pl.BlockSpec(masks.shape, lambda h, i: (0, 0, 0)),
        ],
        out_specs=pl.BlockSpec((c, GLA_DV), lambda h, i: (i, h)),
        out_shape=jax.ShapeDtypeStruct((s, GLA_VV), BF16),
        scratch_shapes=[pltpu.VMEM((GLA_DV, GLA_DK), F32)],
        compiler_params=_params(2),
        name="gla_core",
    )(proj, proj, proj, proj, glr, w_g2p, b_g.reshape(1, GLA_QK), g_o.reshape(1, GLA_DV),
      sums, masks)


def _swa_kernel(sink_ref, q_ref, kvp_ref, kvc_ref, gq_ref, gk_ref, o_ref):
    blk = SWA_BLOCK
    band = jnp.concatenate([kvp_ref[...], kvc_ref[...]], axis=0)
    qi = lax.broadcasted_iota(jnp.int32, (blk, 2 * blk), 0)
    kj = lax.broadcasted_iota(jnp.int32, (blk, 2 * blk), 1)
    rel = qi + blk - kj
    valid = (rel >= 0) & (rel < SWA_WINDOW)
    valid = valid & ((pl.program_id(0) > 0) | (kj >= blk))
    for kvh in range(SWA_KVH):
        k_h = _rmsnorm_rows(band[:, kvh * SWA_HD:(kvh + 1) * SWA_HD], gk_ref[...]).astype(BF16)
        v_h = band[:, SWA_KVW + kvh * SWA_HD:SWA_KVW + (kvh + 1) * SWA_HD].astype(BF16)
        for g in range(SWA_GROUP):
            hd = kvh * SWA_GROUP + g
            q_h = q_ref[:, hd * SWA_HD:(hd + 1) * SWA_HD].astype(F32)
            q_h = (_rmsnorm_rows(q_h, gq_ref[...]) * (SWA_HD ** -0.5)).astype(BF16)
            s = jnp.where(valid, _dot_nt(q_h, k_h), -jnp.inf)
            sink = sink_ref[hd]
            m = jnp.maximum(jnp.max(s, axis=-1, keepdims=True), sink)
            e = jnp.exp(s - m)
            denom = jnp.sum(e, axis=-1, keepdims=True) + jnp.exp(sink - m)
            p = (e / denom).astype(BF16)
            o_h = jnp.dot(p, v_h, preferred_element_type=F32)
            o_ref[:, hd * SWA_HD:(hd + 1) * SWA_HD] = o_h.astype(o_ref.dtype)


def _swa_core(q, kv, g_q, g_k, sinks):
    s = q.shape[0]
    blk = SWA_BLOCK
    return pl.pallas_call(
        _swa_kernel,
        grid=(s // blk,),
        in_specs=[
            pl.BlockSpec(memory_space=pltpu.SMEM),
            pl.BlockSpec((blk, D_MODEL), lambda i: (i, 0)),
            pl.BlockSpec((blk, 2 * SWA_KVW), lambda i: (jnp.maximum(i - 1, 0), 0)),
            pl.BlockSpec((blk, 2 * SWA_KVW), lambda i: (i, 0)),
            pl.BlockSpec((1, SWA_HD), lambda i: (0, 0)),
            pl.BlockSpec((1, SWA_HD), lambda i: (0, 0)),
        ],
        out_specs=pl.BlockSpec((blk, D_MODEL), lambda i: (i, 0)),
        out_shape=jax.ShapeDtypeStruct((s, D_MODEL), BF16),
        compiler_params=_params(1),
        name="swa_core",
    )(sinks, q, kv, kv, g_q.reshape(1, SWA_HD), g_k.reshape(1, SWA_HD))


def kernel(x, norm_mix, norm_mlp, mlp_w1, mlp_w2, a_w_in, a_w_g2, a_b_g, a_g_o, a_w_o, kv_norm,
           kv_w_k, kv_w_v, kv_g_k, b_w_q, b_g_q, b_sinks, b_w_o):
    assert x.shape == (1, SEQ, D_MODEL)
    h = x[0]
    kv = None
    for layer in range(DEPTH):
        if layer < N_GLA:
            w_in = a_w_in[layer]
            w_main = w_in[:, :GLA_MAIN].astype(BF16)
            w_gate = jnp.pad(w_in[:, GLA_MAIN:], ((0, 0), (0, LANES - GLA_RANK))).astype(BF16)
            proj = _norm_matmul(h, norm_mix[layer], w_main, BF16, tm=1024, tn=1024)
            glr = _norm_matmul(h, norm_mix[layer], w_gate, F32, tm=1024, tn=LANES)
            o = _gla_core(proj, glr, a_w_g2[layer], a_b_g[layer], a_g_o[layer])
            h = _matmul_residual(o, a_w_o[layer].astype(BF16), h, tm=1024, tn=1024)
        else:
            j = layer - N_GLA
            if kv is None:
                w_kv = jnp.concatenate([kv_w_k, kv_w_v], axis=1).astype(BF16)
                kv = _norm_matmul(h, kv_norm, w_kv, F32, tm=1024, tn=2 * SWA_KVW)
            q = _norm_matmul(h, norm_mix[layer], b_w_q[j].astype(BF16), BF16, tm=1024, tn=1024)
            o = _swa_core(q, kv, b_g_q[j], kv_g_k, b_sinks[j])
            h = _matmul_residual(o, b_w_o[j].astype(BF16), h, tm=1024, tn=1024)
        h = _mlp(h, norm_mlp[layer], mlp_w1[layer].astype(BF16), mlp_w2[layer].astype(BF16),
                 tm=1024, tf=512)
    return h[None]
```

```python
import functools

import jax
import jax.numpy as jnp
import numpy as np
from jax import lax
from jax.experimental import pallas as pl
from jax.experimental.pallas import tpu as pltpu

F32 = jnp.float32
BF16 = jnp.bfloat16

EPS = 1e-6
D_MODEL = 2048
SEQ = 8192
DEPTH = 4
N_GLA = DEPTH // 2
D_FF = 4 * D_MODEL

GLA_HEADS = 4
GLA_DK = 256
GLA_DV = 512
GLA_RANK = 16
GLA_TAU = 16.0
GLA_QK = GLA_HEADS * GLA_DK
GLA_VV = GLA_HEADS * GLA_DV
GLA_MAIN = 2 * GLA_QK + 2 * GLA_VV
GLA_CHUNK = 256
LANES = 128

SWA_HD = 64
SWA_QH = 32
SWA_KVH = 4
SWA_GROUP = SWA_QH // SWA_KVH
SWA_WINDOW = 128
SWA_BLOCK = 128
SWA_KVW = SWA_KVH * SWA_HD

V7X_VMEM_BYTES = 64 * 1024 * 1024
VMEM_LIMIT = V7X_VMEM_BYTES - 4 * 1024 * 1024


def _params(n_axes):
    return pltpu.CompilerParams(
        dimension_semantics=("arbitrary",) * n_axes, vmem_limit_bytes=VMEM_LIMIT)


def _rmsnorm_rows(x, g):
    ms = jnp.mean(x * x, axis=-1, keepdims=True)
    return x * lax.rsqrt(ms + EPS) * g


def _dot_nt(a, b):
    return lax.dot_general(a, b, (((1,), (1,)), ((), ())), preferred_element_type=F32)


def _dot_tn(a, b):
    return lax.dot_general(a, b, (((0,), (0,)), ((), ())), preferred_element_type=F32)


def _normalize_into(x_ref, g_ref, hn_ref, rows_per_step=256):
    n = x_ref.shape[0] // rows_per_step

    def body(r, carry):
        sl = pl.ds(pl.multiple_of(r * rows_per_step, rows_per_step), rows_per_step)
        hn_ref[sl, :] = _rmsnorm_rows(x_ref[sl, :], g_ref[...]).astype(BF16)
        return carry

    lax.fori_loop(0, n, body, 0)


def _norm_matmul_kernel(x_ref, g_ref, w_ref, o_ref, hn_ref):
    @pl.when(pl.program_id(1) == 0)
    def _():
        _normalize_into(x_ref, g_ref, hn_ref)

    o_ref[...] = jnp.dot(hn_ref[...], w_ref[...], preferred_element_type=F32).astype(o_ref.dtype)


def _norm_matmul(x, g, w, out_dtype, tm, tn):
    s, d = x.shape
    n = w.shape[1]
    return pl.pallas_call(
        _norm_matmul_kernel,
        grid=(s // tm, n // tn),
        in_specs=[
            pl.BlockSpec((tm, d), lambda i, j: (i, 0)),
            pl.BlockSpec((1, d), lambda i, j: (0, 0)),
            pl.BlockSpec((d, tn), lambda i, j: (0, j)),
        ],
        out_specs=pl.BlockSpec((tm, tn), lambda i, j: (i, j)),
        out_shape=jax.ShapeDtypeStruct((s, n), out_dtype),
        scratch_shapes=[pltpu.VMEM((tm, d), BF16)],
        compiler_params=_params(2),
        name="norm_matmul",
    )(x, g.reshape(1, d), w)


def _norm_matmul_t_kernel(x_ref, g_ref, wt_ref, o_ref, hn_ref):
    @pl.when(pl.program_id(1) == 0)
    def _():
        _normalize_into(x_ref, g_ref, hn_ref)

    o_ref[...] = _dot_nt(wt_ref[...], hn_ref[...]).astype(o_ref.dtype)


def _norm_matmul_t(x, g, wt, out_dtype, tm, tn):
    s, d = x.shape
    n = wt.shape[0]
    return pl.pallas_call(
        _norm_matmul_t_kernel,
        grid=(s // tm, n // tn),
        in_specs=[
            pl.BlockSpec((tm, d), lambda i, j: (i, 0)),
            pl.BlockSpec((1, d), lambda i, j: (0, 0)),
            pl.BlockSpec((tn, d), lambda i, j: (j, 0)),
        ],
        out_specs=pl.BlockSpec((tn, tm), lambda i, j: (j, i)),
        out_shape=jax.ShapeDtypeStruct((n, s), out_dtype),
        scratch_shapes=[pltpu.VMEM((tm, d), BF16)],
        compiler_params=_params(2),
        name="norm_matmul_t",
    )(x, g.reshape(1, d), wt)


def _kproj_kernel(x_ref, g_ref, w_ref, gk_ref, o_ref, hn_ref):
    _normalize_into(x_ref, g_ref, hn_ref)
    k = jnp.dot(hn_ref[...], w_ref[...], preferred_element_type=F32)
    heads = [_rmsnorm_rows(k[:, h * SWA_HD:(h + 1) * SWA_HD], gk_ref[...]) for h in range(SWA_KVH)]
    o_ref[...] = jnp.concatenate(heads, axis=1).astype(o_ref.dtype)


def _kproj(x, g, w, g_k, tm):
    s, d = x.shape
    n = w.shape[1]
    return pl.pallas_call(
        _kproj_kernel,
        grid=(s // tm,),
        in_specs=[
            pl.BlockSpec((tm, d), lambda i: (i, 0)),
            pl.BlockSpec((1, d), lambda i: (0, 0)),
            pl.BlockSpec((d, n), lambda i: (0, 0)),
            pl.BlockSpec((1, SWA_HD), lambda i: (0, 0)),
        ],
        out_specs=pl.BlockSpec((tm, n), lambda i: (i, 0)),
        out_shape=jax.ShapeDtypeStruct((s, n), BF16),
        scratch_shapes=[pltpu.VMEM((tm, d), BF16)],
        compiler_params=_params(1),
        name="kproj",
    )(x, g.reshape(1, d), w, g_k.reshape(1, SWA_HD))


def _matmul_t_residual_kernel(at_ref, w_ref, r_ref, o_ref):
    o_ref[...] = r_ref[...] + _dot_tn(at_ref[...], w_ref[...])


def _matmul_t_residual(at, w, res, tm):
    k, s = at.shape
    n = w.shape[1]
    return pl.pallas_call(
        _matmul_t_residual_kernel,
        grid=(s // tm,),
        in_specs=[
            pl.BlockSpec((k, tm), lambda i: (0, i)),
            pl.BlockSpec((k, n), lambda i: (0, 0)),
            pl.BlockSpec((tm, n), lambda i: (i, 0)),
        ],
        out_specs=pl.BlockSpec((tm, n), lambda i: (i, 0)),
        out_shape=jax.ShapeDtypeStruct((s, n), F32),
        compiler_params=_params(1),
        name="matmul_t_residual",
    )(at, w, res)


def _matmul_residual_kernel(a_ref, w_ref, r_ref, o_ref):
    o_ref[...] = r_ref[...] + jnp.dot(a_ref[...], w_ref[...], preferred_element_type=F32)


def _matmul_residual(a, w, res, tm, tn):
    s, k = a.shape
    n = w.shape[1]
    return pl.pallas_call(
        _matmul_residual_kernel,
        grid=(s // tm, n // tn),
        in_specs=[
            pl.BlockSpec((tm, k), lambda i, j: (i, 0)),
            pl.BlockSpec((k, tn), lambda i, j: (0, j)),
            pl.BlockSpec((tm, tn), lambda i, j: (i, j)),
        ],
        out_specs=pl.BlockSpec((tm, tn), lambda i, j: (i, j)),
        out_shape=jax.ShapeDtypeStruct((s, n), F32),
        compiler_params=_params(2),
        name="matmul_residual",
    )(a, w, res)


def _mlp_kernel(x_ref, g_ref, w1_ref, w2_ref, o_ref, hn_ref):
    @pl.when(pl.program_id(1) == 0)
    def _():
        _normalize_into(x_ref, g_ref, hn_ref)
        o_ref[...] = x_ref[...]

    u = jnp.dot(hn_ref[...], w1_ref[...], preferred_element_type=F32)
    u = jnp.maximum(u, 0.0)
    u = (u * u).astype(BF16)
    o_ref[...] += jnp.dot(u, w2_ref[...], preferred_element_type=F32)


def _mlp(x, g, w1, w2, tm, tf):
    s, d = x.shape
    f = w1.shape[1]
    return pl.pallas_call(
        _mlp_kernel,
        grid=(s // tm, f // tf),
        in_specs=[
            pl.BlockSpec((tm, d), lambda i, j: (i, 0)),
            pl.BlockSpec((1, d), lambda i, j: (0, 0)),
            pl.BlockSpec((d, tf), lambda i, j: (0, j)),
            pl.BlockSpec((tf, d), lambda i, j: (j, 0)),
        ],
        out_specs=pl.BlockSpec((tm, d), lambda i, j: (i, 0)),
        out_shape=jax.ShapeDtypeStruct((s, d), F32),
        scratch_shapes=[pltpu.VMEM((tm, d), BF16)],
        compiler_params=_params(2),
        name="mlp",
    )(x, g.reshape(1, d), w1, w2)


@functools.lru_cache(maxsize=None)
def _gla_constants(c):
    i = np.arange(c)[:, None]
    t = np.arange(c)[None, :]
    mats = [t <= i, t > i]
    masks = []
    s = c // 2
    while s >= 1:
        pos = i % (2 * s)
        ref_row = i - pos + s
        upper = pos >= s
        mats.append(np.where(upper, (t > ref_row) & (t <= i), (t > i) & (t <= ref_row)))
        same_block = (i // (2 * s)) == (t // (2 * s))
        masks.append(same_block & upper & ((t % (2 * s)) < s))
        s //= 2
    masks.append(i == t)
    sums = np.concatenate(mats, axis=0).astype(np.float32)
    return sums, np.stack(masks).astype(np.float32)


def _gla_kernel(q_ref, k_ref, v_ref, r_ref, glr_ref, wg_ref, bg_ref, go_ref, sums_ref, mask_ref,
                o_ref, st_ref):
    c = GLA_CHUNK
    n_levels = mask_ref.shape[0] - 1

    @pl.when(pl.program_id(0) == 0)
    def _():
        st_ref[...] = jnp.zeros_like(st_ref)

    gate_in = jnp.dot(glr_ref[...].astype(BF16), wg_ref[...], preferred_element_type=F32)
    gate_in = gate_in + bg_ref[...]
    log_sig = jnp.minimum(gate_in, 0.0) - jnp.log1p(jnp.exp(-jnp.abs(gate_in)))
    log_a = log_sig / GLA_TAU

    la_hi = log_a.astype(BF16)
    la_lo = (log_a - la_hi.astype(F32)).astype(BF16)
    chunk_sums = sums_ref[0:2 * c, :]
    part_chunk = (jnp.dot(chunk_sums, la_hi, preferred_element_type=F32)
                  + jnp.dot(chunk_sums, la_lo, preferred_element_type=F32))
    part_level = jnp.dot(sums_ref[2 * c:, :], la_hi, preferred_element_type=F32)

    for h in range(GLA_HEADS):
        qk_cols = slice(h * GLA_DK, (h + 1) * GLA_DK)
        v_cols = slice(h * GLA_DV, (h + 1) * GLA_DV)
        q = q_ref[:, qk_cols].astype(F32) * (GLA_DK ** -0.5)
        k = k_ref[:, qk_cols].astype(F32)
        v = v_ref[:, v_cols]

        b = part_chunk[0:c, qk_cols]
        state_t = st_ref[h]
        o = _dot_nt((q * jnp.exp(b)).astype(BF16), state_t.astype(BF16))

        attn = _dot_nt(q.astype(BF16), k.astype(BF16)) * mask_ref[n_levels]
        for lvl in range(n_levels):
            e = jnp.exp(part_level[lvl * c:(lvl + 1) * c, qk_cols])
            attn = attn + _dot_nt((q * e).astype(BF16), (k * e).astype(BF16)) * mask_ref[lvl]
        o = o + jnp.dot(attn.astype(BF16), v, preferred_element_type=F32)

        k_dec = (k * jnp.exp(part_chunk[c:2 * c, qk_cols])).astype(BF16)
        st_ref[h] = state_t * jnp.exp(b[c - 1:c, :]) + _dot_tn(v, k_dec)

        o = _rmsnorm_rows(o, go_ref[...])
        r = r_ref[:, v_cols].astype(F32)
        o_ref[:, v_cols] = (o * (r * jax.nn.sigmoid(r))).astype(o_ref.dtype)


def _gla_core(proj, glr, w_g2, b_g, g_o):
    s = proj.shape[0]
    c = GLA_CHUNK
    sums, masks = _gla_constants(c)
    sums = jnp.asarray(sums, dtype=BF16)
    masks = jnp.asarray(masks, dtype=F32)
    w_g2p = jnp.pad(w_g2, ((0, LANES - GLA_RANK), (0, 0))).astype(BF16)
    assert 2 * GLA_QK == GLA_VV
    return pl.pallas_call(
        _gla_kernel,
        grid=(s // c,),
        in_specs=[
            pl.BlockSpec((c, GLA_QK), lambda i: (i, 0)),
            pl.BlockSpec((c, GLA_QK), lambda i: (i, 1)),
            pl.BlockSpec((c, GLA_VV), lambda i: (i, 1)),
            pl.BlockSpec((c, GLA_VV), lambda i: (i, 2)),
            pl.BlockSpec((c, LANES), lambda i: (i, 0)),
            pl.BlockSpec((LANES, GLA_QK), lambda i: (0, 0)),
            pl.BlockSpec((1, GLA_QK), lambda i: (0, 0)),
            pl.BlockSpec((1, GLA_DV), lambda i: (0, 0)),
            pl.BlockSpec(sums.shape, lambda i: (0, 0)),
            pl.BlockSpec(masks.shape, lambda i: (0, 0, 0)),
        ],
        out_specs=pl.BlockSpec((c, GLA_VV), lambda i: (i, 0)),
        out_shape=jax.ShapeDtypeStruct((s, GLA_VV), BF16),
        scratch_shapes=[pltpu.VMEM((GLA_HEADS, GLA_DV, GLA_DK), F32)],
        compiler_params=_params(1),
        name="gla_core",
    )(proj, proj, proj, proj, glr, w_g2p, b_g.reshape(1, GLA_QK), g_o.reshape(1, GLA_DV),
      sums, masks)


@functools.lru_cache(maxsize=None)
def _swa_bias():
    blk = SWA_BLOCK
    kj = np.arange(2 * blk)[:, None]
    qi = np.arange(blk)[None, :]
    rel = qi + blk - kj
    valid = (rel >= 0) & (rel < SWA_WINDOW)
    first = valid & (kj >= blk)
    planes = np.stack([first, valid])
    bias = np.where(planes, 0.0, -np.inf).astype(np.float32)
    return np.tile(bias, (1, 1, SWA_GROUP))


def _swa_kernel(qt_ref, kp_ref, kc_ref, vtp_ref, vtc_ref, gq_ref, sink_ref, bias_ref, o_ref):
    hd = SWA_HD
    blk = SWA_BLOCK
    k_band = jnp.concatenate([kp_ref[...], kc_ref[...]], axis=0)
    vt_band = jnp.concatenate([vtp_ref[...], vtc_ref[...]], axis=1)
    bias = bias_ref[0]
    for kvh in range(SWA_KVH):
        cols = []
        for g in range(SWA_GROUP):
            r0 = (kvh * SWA_GROUP + g) * hd
            q_g = qt_ref[r0:r0 + hd, :].astype(F32)
            ms = jnp.mean(q_g * q_g, axis=0, keepdims=True)
            scale = lax.rsqrt(ms + EPS) * (hd ** -0.5)
            cols.append((q_g * scale * gq_ref[...]).astype(BF16))
        q_cat = jnp.concatenate(cols, axis=1)
        s_t = jnp.dot(k_band[:, kvh * hd:(kvh + 1) * hd], q_cat, preferred_element_type=F32) + bias
        sink = sink_ref[kvh:kvh + 1, :]
        m = jnp.maximum(jnp.max(s_t, axis=0, keepdims=True), sink)
        e = jnp.exp(s_t - m)
        denom = jnp.sum(e, axis=0, keepdims=True) + jnp.exp(sink - m)
        o_t = jnp.dot(vt_band[kvh * hd:(kvh + 1) * hd, :], e.astype(BF16), preferred_element_type=F32)
        o_t = o_t * (1.0 / denom)
        for g in range(SWA_GROUP):
            r0 = (kvh * SWA_GROUP + g) * hd
            o_ref[r0:r0 + hd, :] = o_t[:, g * blk:(g + 1) * blk].astype(o_ref.dtype)


def _swa_core(qt, k, vt, g_q, sinks):
    s = qt.shape[1]
    blk = SWA_BLOCK
    bias = jnp.asarray(_swa_bias())
    gq_col = jnp.broadcast_to(g_q.reshape(SWA_HD, 1), (SWA_HD, blk))
    sink_rows = jnp.repeat(sinks.reshape(SWA_KVH, SWA_GROUP), blk, axis=1)

    def prev(i):
        return jnp.maximum(i - 1, 0)

    return pl.pallas_call(
        _swa_kernel,
        grid=(s // blk,),
        in_specs=[
            pl.BlockSpec((D_MODEL, blk), lambda i: (0, i)),
            pl.BlockSpec((blk, SWA_KVW), lambda i: (prev(i), 0)),
            pl.BlockSpec((blk, SWA_KVW), lambda i: (i, 0)),
            pl.BlockSpec((SWA_KVW, blk), lambda i: (0, prev(i))),
            pl.BlockSpec((SWA_KVW, blk), lambda i: (0, i)),
            pl.BlockSpec((SWA_HD, blk), lambda i: (0, 0)),
            pl.BlockSpec((SWA_KVH, SWA_GROUP * blk), lambda i: (0, 0)),
            pl.BlockSpec((1, 2 * blk, SWA_GROUP * blk), lambda i: (jnp.minimum(i, 1), 0, 0)),
        ],
        out_specs=pl.BlockSpec((D_MODEL, blk), lambda i: (0, i)),
        out_shape=jax.ShapeDtypeStruct((D_MODEL, s), BF16),
        compiler_params=_params(1),
        name="swa_core",
    )(qt, k, k, vt, vt, gq_col, sink_rows, bias)


def kernel(x, norm_mix, norm_mlp, mlp_w1, mlp_w2, a_w_in, a_w_g2, a_b_g, a_g_o, a_w_o, kv_norm,
           kv_w_k, kv_w_v, kv_g_k, b_w_q, b_g_q, b_sinks, b_w_o):
    assert x.shape == (1, SEQ, D_MODEL)
    h = x[0]
    k_shared = vt_shared = None
    for layer in range(DEPTH):
        if layer < N_GLA:
            w_in = a_w_in[layer]
            w_main = w_in[:, :GLA_MAIN].astype(BF16)
            w_gate = jnp.pad(w_in[:, GLA_MAIN:], ((0, 0), (0, LANES - GLA_RANK))).astype(BF16)
            proj = _norm_matmul(h, norm_mix[layer], w_main, BF16, tm=1024, tn=1024)
            glr = _norm_matmul(h, norm_mix[layer], w_gate, F32, tm=1024, tn=LANES)
            o = _gla_core(proj, glr, a_w_g2[layer], a_b_g[layer], a_g_o[layer])
            h = _matmul_residual(o, a_w_o[layer].astype(BF16), h, tm=1024, tn=1024)
        else:
            j = layer - N_GLA
            if k_shared is None:
                k_shared = _kproj(h, kv_norm, kv_w_k.astype(BF16), kv_g_k, tm=1024)
                vt_shared = _norm_matmul_t(h, kv_norm, kv_w_v.T.astype(BF16), BF16, tm=1024,
                                           tn=SWA_KVW)
            qt = _norm_matmul_t(h, norm_mix[layer], b_w_q[j].T.astype(BF16), BF16, tm=1024, tn=1024)
            ot = _swa_core(qt, k_shared, vt_shared, b_g_q[j], b_sinks[j])
            h = _matmul_t_residual(ot, b_w_o[j].astype(BF16), h, tm=512)
        h = _mlp(h, norm_mlp[layer], mlp_w1[layer].astype(BF16), mlp_w2[layer].astype(BF16),
                 tm=1024, tf=1024)
    return h[None]
```

```python
import functools

import jax
import jax.numpy as jnp
import numpy as np
from jax import lax
from jax.experimental import pallas as pl
from jax.experimental.pallas import tpu as pltpu

F32 = jnp.float32
BF16 = jnp.bfloat16

EPS = 1e-6
D_MODEL = 2048
SEQ = 8192
DEPTH = 4
N_GLA = DEPTH // 2
D_FF = 4 * D_MODEL

GLA_HEADS = 4
GLA_DK = 256
GLA_DV = 512
GLA_RANK = 16
GLA_TAU = 16.0
GLA_QK = GLA_HEADS * GLA_DK
GLA_VV = GLA_HEADS * GLA_DV
GLA_MAIN = 2 * GLA_QK + 2 * GLA_VV
GLA_CHUNK = 256
LANES = 128

SWA_HD = 64
SWA_QH = 32
SWA_KVH = 4
SWA_GROUP = SWA_QH // SWA_KVH
SWA_WINDOW = 128
SWA_BLOCK = 128
SWA_KVW = SWA_KVH * SWA_HD

V7X_VMEM_BYTES = 64 * 1024 * 1024
VMEM_LIMIT = V7X_VMEM_BYTES - 4 * 1024 * 1024


def _params(n_axes):
    return pltpu.CompilerParams(
        dimension_semantics=("arbitrary",) * n_axes, vmem_limit_bytes=VMEM_LIMIT)


def _rmsnorm_rows(x, g):
    ms = jnp.mean(x * x, axis=-1, keepdims=True)
    return x * lax.rsqrt(ms + EPS) * g


def _dot_nt(a, b):
    return lax.dot_general(a, b, (((1,), (1,)), ((), ())), preferred_element_type=F32)


def _dot_tn(a, b):
    return lax.dot_general(a, b, (((0,), (0,)), ((), ())), preferred_element_type=F32)


def _normalize_into(x_ref, g_ref, hn_ref, rows_per_step=256):
    n = x_ref.shape[0] // rows_per_step

    def body(r, carry):
        sl = pl.ds(pl.multiple_of(r * rows_per_step, rows_per_step), rows_per_step)
        hn_ref[sl, :] = _rmsnorm_rows(x_ref[sl, :], g_ref[...]).astype(BF16)
        return carry

    lax.fori_loop(0, n, body, 0)


def _weight_tile(w_ref, valid_cols):
    w = w_ref[...]
    if valid_cols is not None:
        col = lax.broadcasted_iota(jnp.int32, w.shape, 1)
        w = jnp.where(col < valid_cols, w, 0.0)
    return w.astype(BF16)


def _norm_matmul_kernel(x_ref, g_ref, w_ref, o_ref, hn_ref, *, valid_cols):
    @pl.when(pl.program_id(1) == 0)
    def _():
        _normalize_into(x_ref, g_ref, hn_ref)

    w = _weight_tile(w_ref, valid_cols)
    o_ref[...] = jnp.dot(hn_ref[...], w, preferred_element_type=F32).astype(o_ref.dtype)


def _norm_matmul(x, g, w, out_dtype, tm, tn, col_block0=0, n_blocks=None, valid_cols=None):
    s, d = x.shape
    if n_blocks is None:
        n_blocks = w.shape[1] // tn
    return pl.pallas_call(
        functools.partial(_norm_matmul_kernel, valid_cols=valid_cols),
        grid=(s // tm, n_blocks),
        in_specs=[
            pl.BlockSpec((tm, d), lambda i, j: (i, 0)),
            pl.BlockSpec((1, d), lambda i, j: (0, 0)),
            pl.BlockSpec((d, tn), lambda i, j: (0, col_block0 + j)),
        ],
        out_specs=pl.BlockSpec((tm, tn), lambda i, j: (i, j)),
        out_shape=jax.ShapeDtypeStruct((s, n_blocks * tn), out_dtype),
        scratch_shapes=[pltpu.VMEM((tm, d), BF16)],
        compiler_params=_params(2),
        name="norm_matmul",
    )(x, g.reshape(1, d), w)


def _norm_matmul_t_kernel(x_ref, g_ref, w_ref, o_ref, hn_ref):
    @pl.when(pl.program_id(1) == 0)
    def _():
        _normalize_into(x_ref, g_ref, hn_ref)

    y_t = lax.dot_general(w_ref[...].astype(BF16), hn_ref[...], (((0,), (1,)), ((), ())),
                          preferred_element_type=F32)
    o_ref[...] = y_t.astype(o_ref.dtype)


def _norm_matmul_t(x, g, w, out_dtype, tm, tn):
    s, d = x.shape
    n = w.shape[1]
    return pl.pallas_call(
        _norm_matmul_t_kernel,
        grid=(s // tm, n // tn),
        in_specs=[
            pl.BlockSpec((tm, d), lambda i, j: (i, 0)),
            pl.BlockSpec((1, d), lambda i, j: (0, 0)),
            pl.BlockSpec((d, tn), lambda i, j: (0, j)),
        ],
        out_specs=pl.BlockSpec((tn, tm), lambda i, j: (j, i)),
        out_shape=jax.ShapeDtypeStruct((n, s), out_dtype),
        scratch_shapes=[pltpu.VMEM((tm, d), BF16)],
        compiler_params=_params(2),
        name="norm_matmul_t",
    )(x, g.reshape(1, d), w)


def _kproj_kernel(x_ref, g_ref, w_ref, gk_ref, o_ref, hn_ref):
    _normalize_into(x_ref, g_ref, hn_ref)
    k = jnp.dot(hn_ref[...], w_ref[...].astype(BF16), preferred_element_type=F32)
    heads = [_rmsnorm_rows(k[:, h * SWA_HD:(h + 1) * SWA_HD], gk_ref[...]) for h in range(SWA_KVH)]
    o_ref[...] = jnp.concatenate(heads, axis=1).astype(o_ref.dtype)


def _kproj(x, g, w, g_k, tm):
    s, d = x.shape
    n = w.shape[1]
    return pl.pallas_call(
        _kproj_kernel,
        grid=(s // tm,),
        in_specs=[
            pl.BlockSpec((tm, d), lambda i: (i, 0)),
            pl.BlockSpec((1, d), lambda i: (0, 0)),
            pl.BlockSpec((d, n), lambda i: (0, 0)),
            pl.BlockSpec((1, SWA_HD), lambda i: (0, 0)),
        ],
        out_specs=pl.BlockSpec((tm, n), lambda i: (i, 0)),
        out_shape=jax.ShapeDtypeStruct((s, n), BF16),
        scratch_shapes=[pltpu.VMEM((tm, d), BF16)],
        compiler_params=_params(1),
        name="kproj",
    )(x, g.reshape(1, d), w, g_k.reshape(1, SWA_HD))


def _matmul_residual_kernel(a_ref, w_ref, r_ref, o_ref, wb_ref, *, a_transposed):
    @pl.when(pl.program_id(1) == 0)
    def _():
        wb_ref[...] = w_ref[...].astype(BF16)

    if a_transposed:
        y = _dot_tn(a_ref[...], wb_ref[...])
    else:
        y = jnp.dot(a_ref[...], wb_ref[...], preferred_element_type=F32)
    o_ref[...] = r_ref[...] + y


def _matmul_residual(a, w, res, tm, tn, a_transposed=False):
    k, n = w.shape
    s = res.shape[0]
    a_spec = (pl.BlockSpec((k, tm), lambda j, i: (0, i)) if a_transposed
              else pl.BlockSpec((tm, k), lambda j, i: (i, 0)))
    return pl.pallas_call(
        functools.partial(_matmul_residual_kernel, a_transposed=a_transposed),
        grid=(n // tn, s // tm),
        in_specs=[
            a_spec,
            pl.BlockSpec((k, tn), lambda j, i: (0, j)),
            pl.BlockSpec((tm, tn), lambda j, i: (i, j)),
        ],
        out_specs=pl.BlockSpec((tm, tn), lambda j, i: (i, j)),
        out_shape=jax.ShapeDtypeStruct((s, n), F32),
        scratch_shapes=[pltpu.VMEM((k, tn), BF16)],
        compiler_params=_params(2),
        name="matmul_residual",
    )(a, w, res)


def _mlp_kernel(x_ref, g_ref, w1_ref, w2_ref, o_ref, hn_ref):
    @pl.when(pl.program_id(1) == 0)
    def _():
        _normalize_into(x_ref, g_ref, hn_ref)
        o_ref[...] = x_ref[...]

    u = jnp.dot(hn_ref[...], w1_ref[...], preferred_element_type=F32)
    u = jnp.maximum(u, 0.0)
    u = (u * u).astype(BF16)
    o_ref[...] += jnp.dot(u, w2_ref[...], preferred_element_type=F32)


def _mlp(x, g, w1, w2, tm, tf):
    s, d = x.shape
    f = w1.shape[1]
    return pl.pallas_call(
        _mlp_kernel,
        grid=(s // tm, f // tf),
        in_specs=[
            pl.BlockSpec((tm, d), lambda i, j: (i, 0)),
            pl.BlockSpec((1, d), lambda i, j: (0, 0)),
            pl.BlockSpec((d, tf), lambda i, j: (0, j)),
            pl.BlockSpec((tf, d), lambda i, j: (j, 0)),
        ],
        out_specs=pl.BlockSpec((tm, d), lambda i, j: (i, 0)),
        out_shape=jax.ShapeDtypeStruct((s, d), F32),
        scratch_shapes=[pltpu.VMEM((tm, d), BF16)],
        compiler_params=_params(2),
        name="mlp",
    )(x, g.reshape(1, d), w1, w2)


@functools.lru_cache(maxsize=None)
def _gla_constants(c):
    i = np.arange(c)[:, None]
    t = np.arange(c)[None, :]
    mats = [t <= i, t > i]
    masks = []
    s = c // 2
    while s >= 1:
        pos = i % (2 * s)
        ref_row = i - pos + s
        upper = pos >= s
        mats.append(np.where(upper, (t > ref_row) & (t <= i), (t > i) & (t <= ref_row)))
        same_block = (i // (2 * s)) == (t // (2 * s))
        masks.append(same_block & upper & ((t % (2 * s)) < s))
        s //= 2
    masks.append(i == t)
    sums = np.concatenate(mats, axis=0).astype(np.float32)
    return sums, np.stack(masks).astype(np.float32)


def _gla_kernel(q_ref, k_ref, v_ref, r_ref, glr_ref, wg_ref, bg_ref, go_ref, sums_ref, mask_ref,
                o_ref, st_ref):
    c = GLA_CHUNK
    n_levels = mask_ref.shape[0] - 1

    @pl.when(pl.program_id(0) == 0)
    def _():
        st_ref[...] = jnp.zeros_like(st_ref)

    gate_in = jnp.dot(glr_ref[...].astype(BF16), wg_ref[...], preferred_element_type=F32)
    gate_in = gate_in + bg_ref[...]
    log_sig = jnp.minimum(gate_in, 0.0) - jnp.log1p(jnp.exp(-jnp.abs(gate_in)))
    log_a = log_sig / GLA_TAU

    la_hi = log_a.astype(BF16)
    la_lo = (log_a - la_hi.astype(F32)).astype(BF16)
    chunk_sums = sums_ref[0:2 * c, :]
    part_chunk = (jnp.dot(chunk_sums, la_hi, preferred_element_type=F32)
                  + jnp.dot(chunk_sums, la_lo, preferred_element_type=F32))
    part_level = jnp.dot(sums_ref[2 * c:, :], la_hi, preferred_element_type=F32)

    for h in range(GLA_HEADS):
        qk_cols = slice(h * GLA_DK, (h + 1) * GLA_DK)
        v_cols = slice(h * GLA_DV, (h + 1) * GLA_DV)
        q = q_ref[:, qk_cols].astype(F32) * (GLA_DK ** -0.5)
        k = k_ref[:, qk_cols].astype(F32)
        v = v_ref[:, v_cols]

        b = part_chunk[0:c, qk_cols]
        state_t = st_ref[h]
        o = _dot_nt((q * jnp.exp(b)).astype(BF16), state_t.astype(BF16))

        attn = _dot_nt(q.astype(BF16), k.astype(BF16)) * mask_ref[n_levels]
        for lvl in range(n_levels):
            e = jnp.exp(part_level[lvl * c:(lvl + 1) * c, qk_cols])
            attn = attn + _dot_nt((q * e).astype(BF16), (k * e).astype(BF16)) * mask_ref[lvl]
        o = o + jnp.dot(attn.astype(BF16), v, preferred_element_type=F32)

        k_dec = (k * jnp.exp(part_chunk[c:2 * c, qk_cols])).astype(BF16)
        st_ref[h] = state_t * jnp.exp(b[c - 1:c, :]) + _dot_tn(v, k_dec)

        o = _rmsnorm_rows(o, go_ref[...])
        r = r_ref[:, v_cols].astype(F32)
        o_ref[:, v_cols] = (o * (r * jax.nn.sigmoid(r))).astype(o_ref.dtype)


def _gla_core(proj, glr, w_g2, b_g, g_o):
    s = proj.shape[0]
    c = GLA_CHUNK
    sums, masks = _gla_constants(c)
    sums = jnp.asarray(sums, dtype=BF16)
    masks = jnp.asarray(masks, dtype=F32)
    w_g2p = jnp.pad(w_g2, ((0, LANES - GLA_RANK), (0, 0))).astype(BF16)
    assert 2 * GLA_QK == GLA_VV
    return pl.pallas_call(
        _gla_kernel,
        grid=(s // c,),
        in_specs=[
            pl.BlockSpec((c, GLA_QK), lambda i: (i, 0)),
            pl.BlockSpec((c, GLA_QK), lambda i: (i, 1)),
            pl.BlockSpec((c, GLA_VV), lambda i: (i, 1)),
            pl.BlockSpec((c, GLA_VV), lambda i: (i, 2)),
            pl.BlockSpec((c, LANES), lambda i: (i, 0)),
            pl.BlockSpec((LANES, GLA_QK), lambda i: (0, 0)),
            pl.BlockSpec((1, GLA_QK), lambda i: (0, 0)),
            pl.BlockSpec((1, GLA_DV), lambda i: (0, 0)),
            pl.BlockSpec(sums.shape, lambda i: (0, 0)),
            pl.BlockSpec(masks.shape, lambda i: (0, 0, 0)),
        ],
        out_specs=pl.BlockSpec((c, GLA_VV), lambda i: (i, 0)),
        out_shape=jax.ShapeDtypeStruct((s, GLA_VV), BF16),
        scratch_shapes=[pltpu.VMEM((GLA_HEADS, GLA_DV, GLA_DK), F32)],
        compiler_params=_params(1),
        name="gla_core",
    )(proj, proj, proj, proj, glr, w_g2p, b_g.reshape(1, GLA_QK), g_o.reshape(1, GLA_DV),
      sums, masks)


@functools.lru_cache(maxsize=None)
def _swa_bias():
    blk = SWA_BLOCK
    kj = np.arange(2 * blk)[:, None]
    qi = np.arange(blk)[None, :]
    rel = qi + blk - kj
    valid = (rel >= 0) & (rel < SWA_WINDOW)
    first = valid & (kj >= blk)
    planes = np.stack([first, valid])
    bias = np.where(planes, 0.0, -np.inf).astype(np.float32)
    return np.tile(bias, (1, 1, SWA_GROUP))


def _swa_kernel(qt_ref, kp_ref, kc_ref, vtp_ref, vtc_ref, gq_ref, sink_ref, bias_ref, o_ref):
    hd = SWA_HD
    blk = SWA_BLOCK
    k_band = jnp.concatenate([kp_ref[...], kc_ref[...]], axis=0)
    vt_band = jnp.concatenate([vtp_ref[...], vtc_ref[...]], axis=1)
    bias = bias_ref[0]
    for kvh in range(SWA_KVH):
        cols = []
        for g in range(SWA_GROUP):
            r0 = (kvh * SWA_GROUP + g) * hd
            q_g = qt_ref[r0:r0 + hd, :].astype(F32)
            ms = jnp.mean(q_g * q_g, axis=0, keepdims=True)
            scale = lax.rsqrt(ms + EPS) * (hd ** -0.5)
            cols.append((q_g * scale * gq_ref[...]).astype(BF16))
        q_cat = jnp.concatenate(cols, axis=1)
        s_t = jnp.dot(k_band[:, kvh * hd:(kvh + 1) * hd], q_cat, preferred_element_type=F32) + bias
        sink = sink_ref[kvh:kvh + 1, :]
        m = jnp.maximum(jnp.max(s_t, axis=0, keepdims=True), sink)
        e = jnp.exp(s_t - m)
        denom = jnp.sum(e, axis=0, keepdims=True) + jnp.exp(sink - m)
        o_t = jnp.dot(vt_band[kvh * hd:(kvh + 1) * hd, :], e.astype(BF16), preferred_element_type=F32)
        o_t = o_t * (1.0 / denom)
        for g in range(SWA_GROUP):
            r0 = (kvh * SWA_GROUP + g) * hd
            o_ref[r0:r0 + hd, :] = o_t[:, g * blk:(g + 1) * blk].astype(o_ref.dtype)


def _swa_core(qt, k, vt, g_q, sinks):
    s = qt.shape[1]
    blk = SWA_BLOCK
    bias = jnp.asarray(_swa_bias())
    gq_col = jnp.broadcast_to(g_q.reshape(SWA_HD, 1), (SWA_HD, blk))
    sink_rows = jnp.repeat(sinks.reshape(SWA_KVH, SWA_GROUP), blk, axis=1)

    def prev(i):
        return jnp.maximum(i - 1, 0)

    return pl.pallas_call(
        _swa_kernel,
        grid=(s // blk,),
        in_specs=[
            pl.BlockSpec((D_MODEL, blk), lambda i: (0, i)),
            pl.BlockSpec((blk, SWA_KVW), lambda i: (prev(i), 0)),
            pl.BlockSpec((blk, SWA_KVW), lambda i: (i, 0)),
            pl.BlockSpec((SWA_KVW, blk), lambda i: (0, prev(i))),
            pl.BlockSpec((SWA_KVW, blk), lambda i: (0, i)),
            pl.BlockSpec((SWA_HD, blk), lambda i: (0, 0)),
            pl.BlockSpec((SWA_KVH, SWA_GROUP * blk), lambda i: (0, 0)),
            pl.BlockSpec((1, 2 * blk, SWA_GROUP * blk), lambda i: (jnp.minimum(i, 1), 0, 0)),
        ],
        out_specs=pl.BlockSpec((D_MODEL, blk), lambda i: (0, i)),
        out_shape=jax.ShapeDtypeStruct((D_MODEL, s), BF16),
        compiler_params=_params(1),
        name="swa_core",
    )(qt, k, k, vt, vt, gq_col, sink_rows, bias)


def kernel(x, norm_mix, norm_mlp, mlp_w1, mlp_w2, a_w_in, a_w_g2, a_b_g, a_g_o, a_w_o, kv_norm,
           kv_w_k, kv_w_v, kv_g_k, b_w_q, b_g_q, b_sinks, b_w_o):
    assert x.shape == (1, SEQ, D_MODEL)
    h = x[0]
    k_shared = vt_shared = None
    for layer in range(DEPTH):
        if layer < N_GLA:
            w_in = a_w_in[layer]
            proj = _norm_matmul(h, norm_mix[layer], w_in, BF16, tm=1024, tn=1024,
                                n_blocks=GLA_MAIN // 1024)
            glr = _norm_matmul(h, norm_mix[layer], w_in, F32, tm=1024, tn=LANES,
                               col_block0=GLA_MAIN // LANES, n_blocks=1, valid_cols=GLA_RANK)
            o = _gla_core(proj, glr, a_w_g2[layer], a_b_g[layer], a_g_o[layer])
            h = _matmul_residual(o, a_w_o[layer], h, tm=1024, tn=1024)
        else:
            j = layer - N_GLA
            if k_shared is None:
                k_shared = _kproj(h, kv_norm, kv_w_k, kv_g_k, tm=1024)
                vt_shared = _norm_matmul_t(h, kv_norm, kv_w_v, BF16, tm=1024, tn=SWA_KVW)
            qt = _norm_matmul_t(h, norm_mix[layer], b_w_q[j], BF16, tm=1024, tn=1024)
            ot = _swa_core(qt, k_shared, vt_shared, b_g_q[j], b_sinks[j])
            h = _matmul_residual(ot, b_w_o[j], h, tm=1024, tn=1024, a_transposed=True)
        h = _mlp(h, norm_mlp[layer], mlp_w1[layer].astype(BF16), mlp_w2[layer].astype(BF16),
                 tm=1024, tf=1024)
    return h[None]
```

```python
import functools

import jax
import jax.numpy as jnp
import numpy as np
from jax import lax
from jax.experimental import pallas as pl
from jax.experimental.pallas import tpu as pltpu

F32 = jnp.float32
BF16 = jnp.bfloat16

EPS = 1e-6
D_MODEL = 2048
SEQ = 8192
DEPTH = 4
N_GLA = DEPTH // 2
D_FF = 4 * D_MODEL

GLA_HEADS = 4
GLA_DK = 256
GLA_DV = 512
GLA_RANK = 16
GLA_TAU = 16.0
GLA_QK = GLA_HEADS * GLA_DK
GLA_VV = GLA_HEADS * GLA_DV
GLA_MAIN = 2 * GLA_QK + 2 * GLA_VV
GLA_CHUNK = 256
LANES = 128

SWA_HD = 64
SWA_QH = 32
SWA_KVH = 4
SWA_GROUP = SWA_QH // SWA_KVH
SWA_WINDOW = 128
SWA_BLOCK = 128
SWA_KVW = SWA_KVH * SWA_HD

V7X_VMEM_BYTES = 64 * 1024 * 1024
VMEM_LIMIT = V7X_VMEM_BYTES - 4 * 1024 * 1024


def _params(n_axes):
    return pltpu.CompilerParams(
        dimension_semantics=("arbitrary",) * n_axes, vmem_limit_bytes=VMEM_LIMIT)


def _rmsnorm_rows(x, g):
    ms = jnp.mean(x * x, axis=-1, keepdims=True)
    return x * lax.rsqrt(ms + EPS) * g


def _dot_nt(a, b):
    return lax.dot_general(a, b, (((1,), (1,)), ((), ())), preferred_element_type=F32)


def _dot_tn(a, b):
    return lax.dot_general(a, b, (((0,), (0,)), ((), ())), preferred_element_type=F32)


def _weight_spec(layer, block, index_map):
    if layer is None:
        return pl.BlockSpec(block, index_map)
    return pl.BlockSpec((None,) + block, lambda *ij: (layer,) + index_map(*ij))


def _normalize_into(x_ref, g_ref, hn_ref, rows_per_step=256):
    n = x_ref.shape[0] // rows_per_step

    def body(r, carry):
        sl = pl.ds(pl.multiple_of(r * rows_per_step, rows_per_step), rows_per_step)
        hn_ref[sl, :] = _rmsnorm_rows(x_ref[sl, :], g_ref[...]).astype(BF16)
        return carry

    lax.fori_loop(0, n, body, 0)


def _weight_tile(w_ref, valid_cols):
    w = w_ref[...]
    if valid_cols is not None:
        col = lax.broadcasted_iota(jnp.int32, w.shape, 1)
        w = jnp.where(col < valid_cols, w, 0.0)
    return w.astype(BF16)


def _norm_matmul_kernel(x_ref, g_ref, w_ref, o_ref, hn_ref, *, valid_cols):
    @pl.when(pl.program_id(1) == 0)
    def _():
        _normalize_into(x_ref, g_ref, hn_ref)

    w = _weight_tile(w_ref, valid_cols)
    o_ref[...] = jnp.dot(hn_ref[...], w, preferred_element_type=F32).astype(o_ref.dtype)


def _norm_matmul(x, g, w, out_dtype, tm, tn, layer=None, col_block0=0, n_blocks=None,
                 valid_cols=None):
    s, d = x.shape
    if n_blocks is None:
        n_blocks = w.shape[-1] // tn
    return pl.pallas_call(
        functools.partial(_norm_matmul_kernel, valid_cols=valid_cols),
        grid=(s // tm, n_blocks),
        in_specs=[
            pl.BlockSpec((tm, d), lambda i, j: (i, 0)),
            pl.BlockSpec((1, d), lambda i, j: (0, 0)),
            _weight_spec(layer, (d, tn), lambda i, j: (0, col_block0 + j)),
        ],
        out_specs=pl.BlockSpec((tm, tn), lambda i, j: (i, j)),
        out_shape=jax.ShapeDtypeStruct((s, n_blocks * tn), out_dtype),
        scratch_shapes=[pltpu.VMEM((tm, d), BF16)],
        compiler_params=_params(2),
        name="norm_matmul",
    )(x, g.reshape(1, d), w)


def _norm_matmul_t_kernel(x_ref, g_ref, w_ref, o_ref, hn_ref):
    @pl.when(pl.program_id(1) == 0)
    def _():
        _normalize_into(x_ref, g_ref, hn_ref)

    y_t = lax.dot_general(w_ref[...].astype(BF16), hn_ref[...], (((0,), (1,)), ((), ())),
                          preferred_element_type=F32)
    o_ref[...] = y_t.astype(o_ref.dtype)


def _norm_matmul_t(x, g, w, out_dtype, tm, tn, layer=None):
    s, d = x.shape
    n = w.shape[-1]
    return pl.pallas_call(
        _norm_matmul_t_kernel,
        grid=(s // tm, n // tn),
        in_specs=[
            pl.BlockSpec((tm, d), lambda i, j: (i, 0)),
            pl.BlockSpec((1, d), lambda i, j: (0, 0)),
            _weight_spec(layer, (d, tn), lambda i, j: (0, j)),
        ],
        out_specs=pl.BlockSpec((tn, tm), lambda i, j: (j, i)),
        out_shape=jax.ShapeDtypeStruct((n, s), out_dtype),
        scratch_shapes=[pltpu.VMEM((tm, d), BF16)],
        compiler_params=_params(2),
        name="norm_matmul_t",
    )(x, g.reshape(1, d), w)


def _kproj_kernel(x_ref, g_ref, w_ref, gk_ref, o_ref, hn_ref):
    _normalize_into(x_ref, g_ref, hn_ref)
    k = jnp.dot(hn_ref[...], w_ref[...].astype(BF16), preferred_element_type=F32)
    heads = [_rmsnorm_rows(k[:, h * SWA_HD:(h + 1) * SWA_HD], gk_ref[...]) for h in range(SWA_KVH)]
    o_ref[...] = jnp.concatenate(heads, axis=1).astype(o_ref.dtype)


def _kproj(x, g, w, g_k, tm):
    s, d = x.shape
    n = w.shape[1]
    return pl.pallas_call(
        _kproj_kernel,
        grid=(s // tm,),
        in_specs=[
            pl.BlockSpec((tm, d), lambda i: (i, 0)),
            pl.BlockSpec((1, d), lambda i: (0, 0)),
            pl.BlockSpec((d, n), lambda i: (0, 0)),
            pl.BlockSpec((1, SWA_HD), lambda i: (0, 0)),
        ],
        out_specs=pl.BlockSpec((tm, n), lambda i: (i, 0)),
        out_shape=jax.ShapeDtypeStruct((s, n), BF16),
        scratch_shapes=[pltpu.VMEM((tm, d), BF16)],
        compiler_params=_params(1),
        name="kproj",
    )(x, g.reshape(1, d), w, g_k.reshape(1, SWA_HD))


def _matmul_residual_kernel(a_ref, w_ref, r_ref, o_ref, wb_ref, *, a_transposed):
    @pl.when(pl.program_id(1) == 0)
    def _():
        wb_ref[...] = w_ref[...].astype(BF16)

    if a_transposed:
        y = _dot_tn(a_ref[...], wb_ref[...])
    else:
        y = jnp.dot(a_ref[...], wb_ref[...], preferred_element_type=F32)
    o_ref[...] = r_ref[...] + y


def _matmul_residual(a, w, res, tm, tn, layer=None, a_transposed=False):
    k, n = w.shape[-2:]
    s = res.shape[0]
    a_spec = (pl.BlockSpec((k, tm), lambda j, i: (0, i)) if a_transposed
              else pl.BlockSpec((tm, k), lambda j, i: (i, 0)))
    return pl.pallas_call(
        functools.partial(_matmul_residual_kernel, a_transposed=a_transposed),
        grid=(n // tn, s // tm),
        in_specs=[
            a_spec,
            _weight_spec(layer, (k, tn), lambda j, i: (0, j)),
            pl.BlockSpec((tm, tn), lambda j, i: (i, j)),
        ],
        out_specs=pl.BlockSpec((tm, tn), lambda j, i: (i, j)),
        out_shape=jax.ShapeDtypeStruct((s, n), F32),
        scratch_shapes=[pltpu.VMEM((k, tn), BF16)],
        compiler_params=_params(2),
        name="matmul_residual",
    )(a, w, res)


def _mlp_kernel(x_ref, g_ref, w1_ref, w2_ref, o_ref, hn_ref):
    @pl.when(pl.program_id(1) == 0)
    def _():
        _normalize_into(x_ref, g_ref, hn_ref)
        o_ref[...] = x_ref[...]

    u = jnp.dot(hn_ref[...], w1_ref[...].astype(BF16), preferred_element_type=F32)
    u = jnp.maximum(u, 0.0)
    u = (u * u).astype(BF16)
    o_ref[...] += jnp.dot(u, w2_ref[...].astype(BF16), preferred_element_type=F32)


def _mlp(x, g, w1, w2, tm, tf, layer=None):
    s, d = x.shape
    f = w1.shape[-1]
    return pl.pallas_call(
        _mlp_kernel,
        grid=(s // tm, f // tf),
        in_specs=[
            pl.BlockSpec((tm, d), lambda i, j: (i, 0)),
            pl.BlockSpec((1, d), lambda i, j: (0, 0)),
            _weight_spec(layer, (d, tf), lambda i, j: (0, j)),
            _weight_spec(layer, (tf, d), lambda i, j: (j, 0)),
        ],
        out_specs=pl.BlockSpec((tm, d), lambda i, j: (i, 0)),
        out_shape=jax.ShapeDtypeStruct((s, d), F32),
        scratch_shapes=[pltpu.VMEM((tm, d), BF16)],
        compiler_params=_params(2),
        name="mlp",
    )(x, g.reshape(1, d), w1, w2)


@functools.lru_cache(maxsize=None)
def _gla_constants(c):
    i = np.arange(c)[:, None]
    t = np.arange(c)[None, :]
    mats = [t <= i, t > i]
    masks = []
    s = c // 2
    while s >= 1:
        pos = i % (2 * s)
        ref_row = i - pos + s
        upper = pos >= s
        mats.append(np.where(upper, (t > ref_row) & (t <= i), (t > i) & (t <= ref_row)))
        same_block = (i // (2 * s)) == (t // (2 * s))
        masks.append(same_block & upper & ((t % (2 * s)) < s))
        s //= 2
    masks.append(i == t)
    sums = np.concatenate(mats, axis=0).astype(np.float32)
    return sums, np.stack(masks).astype(np.float32)


def _gla_kernel(q_ref, k_ref, v_ref, r_ref, glr_ref, wg_ref, bg_ref, go_ref, sums_ref, mask_ref,
                o_ref, st_ref):
    c = GLA_CHUNK
    n_levels = mask_ref.shape[0] - 1

    @pl.when(pl.program_id(0) == 0)
    def _():
        st_ref[...] = jnp.zeros_like(st_ref)

    gate_in = jnp.dot(glr_ref[...].astype(BF16), wg_ref[...], preferred_element_type=F32)
    gate_in = gate_in + bg_ref[...]
    log_sig = jnp.minimum(gate_in, 0.0) - jnp.log1p(jnp.exp(-jnp.abs(gate_in)))
    log_a = log_sig / GLA_TAU

    la_hi = log_a.astype(BF16)
    la_lo = (log_a - la_hi.astype(F32)).astype(BF16)
    chunk_sums = sums_ref[0:2 * c, :]
    part_chunk = (jnp.dot(chunk_sums, la_hi, preferred_element_type=F32)
                  + jnp.dot(chunk_sums, la_lo, preferred_element_type=F32))
    part_level = jnp.dot(sums_ref[2 * c:, :], la_hi, preferred_element_type=F32)

    for h in range(GLA_HEADS):
        qk_cols = slice(h * GLA_DK, (h + 1) * GLA_DK)
        v_cols = slice(h * GLA_DV, (h + 1) * GLA_DV)
        q = q_ref[:, qk_cols].astype(F32) * (GLA_DK ** -0.5)
        k = k_ref[:, qk_cols].astype(F32)
        v = v_ref[:, v_cols]

        b = part_chunk[0:c, qk_cols]
        state_t = st_ref[h]
        o = _dot_nt((q * jnp.exp(b)).astype(BF16), state_t.astype(BF16))

        attn = _dot_nt(q.astype(BF16), k.astype(BF16)) * mask_ref[n_levels]
        for lvl in range(n_levels):
            e = jnp.exp(part_level[lvl * c:(lvl + 1) * c, qk_cols])
            attn = attn + _dot_nt((q * e).astype(BF16), (k * e).astype(BF16)) * mask_ref[lvl]
        o = o + jnp.dot(attn.astype(BF16), v, preferred_element_type=F32)

        k_dec = (k * jnp.exp(part_chunk[c:2 * c, qk_cols])).astype(BF16)
        st_ref[h] = state_t * jnp.exp(b[c - 1:c, :]) + _dot_tn(v, k_dec)

        o = _rmsnorm_rows(o, go_ref[...])
        r = r_ref[:, v_cols].astype(F32)
        o_ref[:, v_cols] = (o * (r * jax.nn.sigmoid(r))).astype(o_ref.dtype)


def _gla_core(proj, glr, w_g2, b_g, g_o):
    s = proj.shape[0]
    c = GLA_CHUNK
    sums, masks = _gla_constants(c)
    sums = jnp.asarray(sums, dtype=BF16)
    masks = jnp.asarray(masks, dtype=F32)
    w_g2p = jnp.pad(w_g2, ((0, LANES - GLA_RANK), (0, 0))).astype(BF16)
    assert 2 * GLA_QK == GLA_VV
    return pl.pallas_call(
        _gla_kernel,
        grid=(s // c,),
        in_specs=[
            pl.BlockSpec((c, GLA_QK), lambda i: (i, 0)),
            pl.BlockSpec((c, GLA_QK), lambda i: (i, 1)),
            pl.BlockSpec((c, GLA_VV), lambda i: (i, 1)),
            pl.BlockSpec((c, GLA_VV), lambda i: (i, 2)),
            pl.BlockSpec((c, LANES), lambda i: (i, 0)),
            pl.BlockSpec((LANES, GLA_QK), lambda i: (0, 0)),
            pl.BlockSpec((1, GLA_QK), lambda i: (0, 0)),
            pl.BlockSpec((1, GLA_DV), lambda i: (0, 0)),
            pl.BlockSpec(sums.shape, lambda i: (0, 0)),
            pl.BlockSpec(masks.shape, lambda i: (0, 0, 0)),
        ],
        out_specs=pl.BlockSpec((c, GLA_VV), lambda i: (i, 0)),
        out_shape=jax.ShapeDtypeStruct((s, GLA_VV), BF16),
        scratch_shapes=[pltpu.VMEM((GLA_HEADS, GLA_DV, GLA_DK), F32)],
        compiler_params=_params(1),
        name="gla_core",
    )(proj, proj, proj, proj, glr, w_g2p, b_g.reshape(1, GLA_QK), g_o.reshape(1, GLA_DV),
      sums, masks)


@functools.lru_cache(maxsize=None)
def _swa_bias():
    blk = SWA_BLOCK
    kj = np.arange(2 * blk)[:, None]
    qi = np.arange(blk)[None, :]
    rel = qi + blk - kj
    valid = (rel >= 0) & (rel < SWA_WINDOW)
    first = valid & (kj >= blk)
    planes = np.stack([first, valid])
    bias = np.where(planes, 0.0, -np.inf).astype(np.float32)
    return np.tile(bias, (1, 1, SWA_GROUP))


def _swa_kernel(qt_ref, kp_ref, kc_ref, vtp_ref, vtc_ref, gq_ref, sink_ref, bias_ref, o_ref):
    hd = SWA_HD
    blk = SWA_BLOCK
    k_band = jnp.concatenate([kp_ref[...], kc_ref[...]], axis=0)
    vt_band = jnp.concatenate([vtp_ref[...], vtc_ref[...]], axis=1)
    bias = bias_ref[0]
    for kvh in range(SWA_KVH):
        cols = []
        for g in range(SWA_GROUP):
            r0 = (kvh * SWA_GROUP + g) * hd
            q_g = qt_ref[r0:r0 + hd, :].astype(F32)
            ms = jnp.mean(q_g * q_g, axis=0, keepdims=True)
            scale = lax.rsqrt(ms + EPS) * (hd ** -0.5)
            cols.append((q_g * scale * gq_ref[...]).astype(BF16))
        q_cat = jnp.concatenate(cols, axis=1)
        s_t = jnp.dot(k_band[:, kvh * hd:(kvh + 1) * hd], q_cat, preferred_element_type=F32) + bias
        sink = sink_ref[kvh:kvh + 1, :]
        m = jnp.maximum(jnp.max(s_t, axis=0, keepdims=True), sink)
        e = jnp.exp(s_t - m)
        denom = jnp.sum(e, axis=0, keepdims=True) + jnp.exp(sink - m)
        o_t = jnp.dot(vt_band[kvh * hd:(kvh + 1) * hd, :], e.astype(BF16), preferred_element_type=F32)
        o_t = o_t * (1.0 / denom)
        for g in range(SWA_GROUP):
            r0 = (kvh * SWA_GROUP + g) * hd
            o_ref[r0:r0 + hd, :] = o_t[:, g * blk:(g + 1) * blk].astype(o_ref.dtype)


def _swa_core(qt, k, vt, g_q, sinks):
    s = qt.shape[1]
    blk = SWA_BLOCK
    bias = jnp.asarray(_swa_bias())
    gq_col = jnp.broadcast_to(g_q.reshape(SWA_HD, 1), (SWA_HD, blk))
    sink_rows = jnp.repeat(sinks.reshape(SWA_KVH, SWA_GROUP), blk, axis=1)

    def prev(i):
        return jnp.maximum(i - 1, 0)

    return pl.pallas_call(
        _swa_kernel,
        grid=(s // blk,),
        in_specs=[
            pl.BlockSpec((D_MODEL, blk), lambda i: (0, i)),
            pl.BlockSpec((blk, SWA_KVW), lambda i: (prev(i), 0)),
            pl.BlockSpec((blk, SWA_KVW), lambda i: (i, 0)),
            pl.BlockSpec((SWA_KVW, blk), lambda i: (0, prev(i))),
            pl.BlockSpec((SWA_KVW, blk), lambda i: (0, i)),
            pl.BlockSpec((SWA_HD, blk), lambda i: (0, 0)),
            pl.BlockSpec((SWA_KVH, SWA_GROUP * blk), lambda i: (0, 0)),
            pl.BlockSpec((1, 2 * blk, SWA_GROUP * blk), lambda i: (jnp.minimum(i, 1), 0, 0)),
        ],
        out_specs=pl.BlockSpec((D_MODEL, blk), lambda i: (0, i)),
        out_shape=jax.ShapeDtypeStruct((D_MODEL, s), BF16),
        compiler_params=_params(1),
        name="swa_core",
    )(qt, k, k, vt, vt, gq_col, sink_rows, bias)


def kernel(x, norm_mix, norm_mlp, mlp_w1, mlp_w2, a_w_in, a_w_g2, a_b_g, a_g_o, a_w_o, kv_norm,
           kv_w_k, kv_w_v, kv_g_k, b_w_q, b_g_q, b_sinks, b_w_o):
    assert x.shape == (1, SEQ, D_MODEL)
    h = x[0]
    k_shared = vt_shared = None
    for layer in range(DEPTH):
        if layer < N_GLA:
            proj = _norm_matmul(h, norm_mix[layer], a_w_in, BF16, tm=1024, tn=1024, layer=layer,
                                n_blocks=GLA_MAIN // 1024)
            glr = _norm_matmul(h, norm_mix[layer], a_w_in, F32, tm=1024, tn=LANES, layer=layer,
                               col_block0=GLA_MAIN // LANES, n_blocks=1, valid_cols=GLA_RANK)
            o = _gla_core(proj, glr, a_w_g2[layer], a_b_g[layer], a_g_o[layer])
            h = _matmul_residual(o, a_w_o, h, tm=1024, tn=1024, layer=layer)
        else:
            j = layer - N_GLA
            if k_shared is None:
                k_shared = _kproj(h, kv_norm, kv_w_k, kv_g_k, tm=1024)
                vt_shared = _norm_matmul_t(h, kv_norm, kv_w_v, BF16, tm=1024, tn=SWA_KVW)
            qt = _norm_matmul_t(h, norm_mix[layer], b_w_q, BF16, tm=1024, tn=1024, layer=j)
            ot = _swa_core(qt, k_shared, vt_shared, b_g_q[j], b_sinks[j])
            h = _matmul_residual(ot, b_w_o, h, tm=1024, tn=1024, layer=j, a_transposed=True)
        h = _mlp(h, norm_mlp[layer], mlp_w1, mlp_w2, tm=1024, tf=512, layer=layer)
    return h[None]
```

```python
import functools

import jax
import jax.numpy as jnp
import numpy as np
from jax import lax
from jax.experimental import pallas as pl
from jax.experimental.pallas import tpu as pltpu

F32 = jnp.float32
BF16 = jnp.bfloat16

EPS = 1e-6
D_MODEL = 2048
SEQ = 8192
DEPTH = 4
N_GLA = DEPTH // 2
D_FF = 4 * D_MODEL

GLA_HEADS = 4
GLA_DK = 256
GLA_DV = 512
GLA_RANK = 16
GLA_TAU = 16.0
GLA_QK = GLA_HEADS * GLA_DK
GLA_VV = GLA_HEADS * GLA_DV
GLA_MAIN = 2 * GLA_QK + 2 * GLA_VV
GLA_CHUNK = 256
LANES = 128

SWA_HD = 64
SWA_QH = 32
SWA_KVH = 4
SWA_GROUP = SWA_QH // SWA_KVH
SWA_WINDOW = 128
SWA_BLOCK = 128
SWA_KVW = SWA_KVH * SWA_HD

V7X_VMEM_BYTES = 64 * 1024 * 1024
VMEM_LIMIT = V7X_VMEM_BYTES - 4 * 1024 * 1024


def _params(n_axes):
    return pltpu.CompilerParams(
        dimension_semantics=("arbitrary",) * n_axes, vmem_limit_bytes=VMEM_LIMIT)


def _rmsnorm_rows(x, g):
    ms = jnp.mean(x * x, axis=-1, keepdims=True)
    return x * lax.rsqrt(ms + EPS) * g


def _dot_nt(a, b):
    return lax.dot_general(a, b, (((1,), (1,)), ((), ())), preferred_element_type=F32)


def _dot_tn(a, b):
    return lax.dot_general(a, b, (((0,), (0,)), ((), ())), preferred_element_type=F32)


def _weight_spec(layer, block, index_map):
    if layer is None:
        return pl.BlockSpec(block, index_map)
    return pl.BlockSpec((None,) + block, lambda *ij: (layer,) + index_map(*ij))


def _normalize_into(x_ref, g_ref, hn_ref, rows_per_step=256):
    n = x_ref.shape[0] // rows_per_step

    def body(r, carry):
        sl = pl.ds(pl.multiple_of(r * rows_per_step, rows_per_step), rows_per_step)
        hn_ref[sl, :] = _rmsnorm_rows(x_ref[sl, :], g_ref[...]).astype(BF16)
        return carry

    lax.fori_loop(0, n, body, 0)


def _dot_wt(w, hn):
    return lax.dot_general(w, hn, (((0,), (1,)), ((), ())), preferred_element_type=F32)


def _gla_inproj_kernel(x_ref, g_ref, wt_ref, wgt_ref, o_ref, glr_ref, hn_ref):
    @pl.when(pl.program_id(1) == 0)
    def _():
        _normalize_into(x_ref, g_ref, hn_ref)
        wg = wgt_ref[...]
        row = lax.broadcasted_iota(jnp.int32, wg.shape, 0)
        wg = jnp.where(row < GLA_RANK, wg, 0.0).astype(BF16)
        glr_ref[...] = _dot_nt(hn_ref[...], wg)

    o_ref[...] = _dot_nt(hn_ref[...], wt_ref[...].astype(BF16)).astype(o_ref.dtype)


def _gla_inproj(x, g, wt, layer, tm, tn):
    s, d = x.shape
    return pl.pallas_call(
        _gla_inproj_kernel,
        grid=(s // tm, GLA_MAIN // tn),
        in_specs=[
            pl.BlockSpec((tm, d), lambda i, j: (i, 0)),
            pl.BlockSpec((1, d), lambda i, j: (0, 0)),
            pl.BlockSpec((None, tn, d), lambda i, j: (layer, j, 0)),
            pl.BlockSpec((None, LANES, d), lambda i, j: (layer, GLA_MAIN // LANES, 0)),
        ],
        out_specs=[
            pl.BlockSpec((tm, tn), lambda i, j: (i, j)),
            pl.BlockSpec((tm, LANES), lambda i, j: (i, 0)),
        ],
        out_shape=[
            jax.ShapeDtypeStruct((s, GLA_MAIN), BF16),
            jax.ShapeDtypeStruct((s, LANES), F32),
        ],
        scratch_shapes=[pltpu.VMEM((tm, d), BF16)],
        compiler_params=_params(2),
        name="gla_inproj",
    )(x, g.reshape(1, d), wt, wt)


def _swa_proj_kernel(*refs, with_kv):
    if with_kv:
        (x_ref, gq_ref, wq_ref, gkv_ref, wk_ref, wv_ref, gk_ref,
         qt_ref, k_ref, vt_ref, hnq_ref, hnkv_ref) = refs
    else:
        x_ref, gq_ref, wq_ref, qt_ref, hnq_ref = refs

    @pl.when(pl.program_id(1) == 0)
    def _():
        _normalize_into(x_ref, gq_ref, hnq_ref)
        if with_kv:
            _normalize_into(x_ref, gkv_ref, hnkv_ref)
            k = jnp.dot(hnkv_ref[...], wk_ref[...].astype(BF16), preferred_element_type=F32)
            heads = [_rmsnorm_rows(k[:, h * SWA_HD:(h + 1) * SWA_HD], gk_ref[...])
                     for h in range(SWA_KVH)]
            k_ref[...] = jnp.concatenate(heads, axis=1).astype(k_ref.dtype)
            vt_ref[...] = _dot_wt(wv_ref[...].astype(BF16), hnkv_ref[...]).astype(vt_ref.dtype)

    qt_ref[...] = _dot_wt(wq_ref[...].astype(BF16), hnq_ref[...]).astype(qt_ref.dtype)


def _swa_proj(x, g_q, w_q, layer, tm, tn, kv=None):
    s, d = x.shape
    n = w_q.shape[-1]
    in_specs = [
        pl.BlockSpec((tm, d), lambda i, j: (i, 0)),
        pl.BlockSpec((1, d), lambda i, j: (0, 0)),
        pl.BlockSpec((None, d, tn), lambda i, j: (layer, 0, j)),
    ]
    args = [x, g_q.reshape(1, d), w_q]
    out_specs = [pl.BlockSpec((tn, tm), lambda i, j: (j, i))]
    out_shape = [jax.ShapeDtypeStruct((n, s), BF16)]
    scratch = [pltpu.VMEM((tm, d), BF16)]
    if kv is not None:
        g_kv, w_k, w_v, g_k = kv
        in_specs += [
            pl.BlockSpec((1, d), lambda i, j: (0, 0)),
            pl.BlockSpec((d, SWA_KVW), lambda i, j: (0, 0)),
            pl.BlockSpec((d, SWA_KVW), lambda i, j: (0, 0)),
            pl.BlockSpec((1, SWA_HD), lambda i, j: (0, 0)),
        ]
        args += [g_kv.reshape(1, d), w_k, w_v, g_k.reshape(1, SWA_HD)]
        out_specs += [
            pl.BlockSpec((tm, SWA_KVW), lambda i, j: (i, 0)),
            pl.BlockSpec((SWA_KVW, tm), lambda i, j: (0, i)),
        ]
        out_shape += [
            jax.ShapeDtypeStruct((s, SWA_KVW), BF16),
            jax.ShapeDtypeStruct((SWA_KVW, s), BF16),
        ]
        scratch += [pltpu.VMEM((tm, d), BF16)]
    return pl.pallas_call(
        functools.partial(_swa_proj_kernel, with_kv=kv is not None),
        grid=(s // tm, n // tn),
        in_specs=in_specs,
        out_specs=out_specs,
        out_shape=out_shape,
        scratch_shapes=scratch,
        compiler_params=_params(2),
        name="swa_proj",
    )(*args)


def _matmul_residual_kernel(a_ref, w_ref, r_ref, o_ref, wb_ref, *, a_transposed):
    @pl.when(pl.program_id(1) == 0)
    def _():
        wb_ref[...] = w_ref[...].astype(BF16)

    if a_transposed:
        y = _dot_tn(a_ref[...], wb_ref[...])
    else:
        y = jnp.dot(a_ref[...], wb_ref[...], preferred_element_type=F32)
    o_ref[...] = r_ref[...] + y


def _matmul_residual(a, w, res, tm, tn, layer=None, a_transposed=False):
    k, n = w.shape[-2:]
    s = res.shape[0]
    a_spec = (pl.BlockSpec((k, tm), lambda j, i: (0, i)) if a_transposed
              else pl.BlockSpec((tm, k), lambda j, i: (i, 0)))
    return pl.pallas_call(
        functools.partial(_matmul_residual_kernel, a_transposed=a_transposed),
        grid=(n // tn, s // tm),
        in_specs=[
            a_spec,
            _weight_spec(layer, (k, tn), lambda j, i: (0, j)),
            pl.BlockSpec((tm, tn), lambda j, i: (i, j)),
        ],
        out_specs=pl.BlockSpec((tm, tn), lambda j, i: (i, j)),
        out_shape=jax.ShapeDtypeStruct((s, n), F32),
        scratch_shapes=[pltpu.VMEM((k, tn), BF16)],
        compiler_params=_params(2),
        name="matmul_residual",
    )(a, w, res)


def _mlp_kernel(x_ref, g_ref, w1_ref, w2_ref, o_ref, hn_ref):
    @pl.when(pl.program_id(1) == 0)
    def _():
        _normalize_into(x_ref, g_ref, hn_ref)
        o_ref[...] = x_ref[...]

    u = jnp.dot(hn_ref[...], w1_ref[...].astype(BF16), preferred_element_type=F32)
    u = jnp.maximum(u, 0.0)
    u = (u * u).astype(BF16)
    o_ref[...] += jnp.dot(u, w2_ref[...].astype(BF16), preferred_element_type=F32)


def _mlp(x, g, w1, w2, tm, tf, layer=None):
    s, d = x.shape
    f = w1.shape[-1]
    return pl.pallas_call(
        _mlp_kernel,
        grid=(s // tm, f // tf),
        in_specs=[
            pl.BlockSpec((tm, d), lambda i, j: (i, 0)),
            pl.BlockSpec((1, d), lambda i, j: (0, 0)),
            _weight_spec(layer, (d, tf), lambda i, j: (0, j)),
            _weight_spec(layer, (tf, d), lambda i, j: (j, 0)),
        ],
        out_specs=pl.BlockSpec((tm, d), lambda i, j: (i, 0)),
        out_shape=jax.ShapeDtypeStruct((s, d), F32),
        scratch_shapes=[pltpu.VMEM((tm, d), BF16)],
        compiler_params=_params(2),
        name="mlp",
    )(x, g.reshape(1, d), w1, w2)


@functools.lru_cache(maxsize=None)
def _gla_constants(c):
    i = np.arange(c)[:, None]
    t = np.arange(c)[None, :]
    mats = [t <= i, t > i]
    masks = []
    s = c // 2
    while s >= 1:
        pos = i % (2 * s)
        ref_row = i - pos + s
        upper = pos >= s
        mats.append(np.where(upper, (t > ref_row) & (t <= i), (t > i) & (t <= ref_row)))
        same_block = (i // (2 * s)) == (t // (2 * s))
        masks.append(same_block & upper & ((t % (2 * s)) < s))
        s //= 2
    masks.append(i == t)
    sums = np.concatenate(mats, axis=0).astype(np.float32)
    return sums, np.stack(masks).astype(np.float32)


def _gla_kernel(q_ref, k_ref, v_ref, r_ref, glr_ref, wg_ref, bg_ref, go_ref, sums_ref, mask_ref,
                o_ref, st_ref):
    c = GLA_CHUNK
    n_levels = mask_ref.shape[0] - 1

    @pl.when(pl.program_id(0) == 0)
    def _():
        st_ref[...] = jnp.zeros_like(st_ref)

    gate_in = jnp.dot(glr_ref[...].astype(BF16), wg_ref[...], preferred_element_type=F32)
    gate_in = gate_in + bg_ref[...]
    log_sig = jnp.minimum(gate_in, 0.0) - jnp.log1p(jnp.exp(-jnp.abs(gate_in)))
    log_a = log_sig / GLA_TAU

    la_hi = log_a.astype(BF16)
    la_lo = (log_a - la_hi.astype(F32)).astype(BF16)
    chunk_sums = sums_ref[0:2 * c, :]
    part_chunk = (jnp.dot(chunk_sums, la_hi, preferred_element_type=F32)
                  + jnp.dot(chunk_sums, la_lo, preferred_element_type=F32))
    part_level = jnp.dot(sums_ref[2 * c:, :], la_hi, preferred_element_type=F32)

    for h in range(GLA_HEADS):
        qk_cols = slice(h * GLA_DK, (h + 1) * GLA_DK)
        v_cols = slice(h * GLA_DV, (h + 1) * GLA_DV)
        q = q_ref[:, qk_cols].astype(F32) * (GLA_DK ** -0.5)
        k = k_ref[:, qk_cols].astype(F32)
        v = v_ref[:, v_cols]

        b = part_chunk[0:c, qk_cols]
        state_t = st_ref[h]
        o = _dot_nt((q * jnp.exp(b)).astype(BF16), state_t.astype(BF16))

        attn = _dot_nt(q.astype(BF16), k.astype(BF16)) * mask_ref[n_levels]
        for lvl in range(n_levels):
            e = jnp.exp(part_level[lvl * c:(lvl + 1) * c, qk_cols])
            attn = attn + _dot_nt((q * e).astype(BF16), (k * e).astype(BF16)) * mask_ref[lvl]
        o = o + jnp.dot(attn.astype(BF16), v, preferred_element_type=F32)

        k_dec = (k * jnp.exp(part_chunk[c:2 * c, qk_cols])).astype(BF16)
        st_ref[h] = state_t * jnp.exp(b[c - 1:c, :]) + _dot_tn(v, k_dec)

        o = _rmsnorm_rows(o, go_ref[...])
        r = r_ref[:, v_cols].astype(F32)
        o_ref[:, v_cols] = (o * (r * jax.nn.sigmoid(r))).astype(o_ref.dtype)


def _gla_core(proj, glr, w_g2, b_g, g_o):
    s = proj.shape[0]
    c = GLA_CHUNK
    sums, masks = _gla_constants(c)
    sums = jnp.asarray(sums, dtype=BF16)
    masks = jnp.asarray(masks, dtype=F32)
    w_g2p = jnp.pad(w_g2, ((0, LANES - GLA_RANK), (0, 0))).astype(BF16)
    assert 2 * GLA_QK == GLA_VV
    return pl.pallas_call(
        _gla_kernel,
        grid=(s // c,),
        in_specs=[
            pl.BlockSpec((c, GLA_QK), lambda i: (i, 0)),
            pl.BlockSpec((c, GLA_QK), lambda i: (i, 1)),
            pl.BlockSpec((c, GLA_VV), lambda i: (i, 1)),
            pl.BlockSpec((c, GLA_VV), lambda i: (i, 2)),
            pl.BlockSpec((c, LANES), lambda i: (i, 0)),
            pl.BlockSpec((LANES, GLA_QK), lambda i: (0, 0)),
            pl.BlockSpec((1, GLA_QK), lambda i: (0, 0)),
            pl.BlockSpec((1, GLA_DV), lambda i: (0, 0)),
            pl.BlockSpec(sums.shape, lambda i: (0, 0)),
            pl.BlockSpec(masks.shape, lambda i: (0, 0, 0)),
        ],
        out_specs=pl.BlockSpec((c, GLA_VV), lambda i: (i, 0)),
        out_shape=jax.ShapeDtypeStruct((s, GLA_VV), BF16),
        scratch_shapes=[pltpu.VMEM((GLA_HEADS, GLA_DV, GLA_DK), F32)],
        compiler_params=_params(1),
        name="gla_core",
    )(proj, proj, proj, proj, glr, w_g2p, b_g.reshape(1, GLA_QK), g_o.reshape(1, GLA_DV),
      sums, masks)


@functools.lru_cache(maxsize=None)
def _swa_bias():
    blk = SWA_BLOCK
    kj = np.arange(2 * blk)[:, None]
    qi = np.arange(blk)[None, :]
    rel = qi + blk - kj
    valid = (rel >= 0) & (rel < SWA_WINDOW)
    first = valid & (kj >= blk)
    planes = np.stack([first, valid])
    bias = np.where(planes, 0.0, -np.inf).astype(np.float32)
    return np.tile(bias, (1, 1, SWA_GROUP))


def _swa_kernel(qt_ref, kp_ref, kc_ref, vtp_ref, vtc_ref, gq_ref, sink_ref, bias_ref, o_ref):
    hd = SWA_HD
    blk = SWA_BLOCK
    k_band = jnp.concatenate([kp_ref[...], kc_ref[...]], axis=0)
    vt_band = jnp.concatenate([vtp_ref[...], vtc_ref[...]], axis=1)
    bias = bias_ref[0]
    for kvh in range(SWA_KVH):
        cols = []
        for g in range(SWA_GROUP):
            r0 = (kvh * SWA_GROUP + g) * hd
            q_g = qt_ref[r0:r0 + hd, :].astype(F32)
            ms = jnp.mean(q_g * q_g, axis=0, keepdims=True)
            scale = lax.rsqrt(ms + EPS) * (hd ** -0.5)
            cols.append((q_g * scale * gq_ref[...]).astype(BF16))
        q_cat = jnp.concatenate(cols, axis=1)
        s_t = jnp.dot(k_band[:, kvh * hd:(kvh + 1) * hd], q_cat, preferred_element_type=F32) + bias
        sink = sink_ref[kvh:kvh + 1, :]
        m = jnp.maximum(jnp.max(s_t, axis=0, keepdims=True), sink)
        e = jnp.exp(s_t - m)
        denom = jnp.sum(e, axis=0, keepdims=True) + jnp.exp(sink - m)
        o_t = jnp.dot(vt_band[kvh * hd:(kvh + 1) * hd, :], e.astype(BF16), preferred_element_type=F32)
        o_t = o_t * (1.0 / denom)
        for g in range(SWA_GROUP):
            r0 = (kvh * SWA_GROUP + g) * hd
            o_ref[r0:r0 + hd, :] = o_t[:, g * blk:(g + 1) * blk].astype(o_ref.dtype)


def _swa_core(qt, k, vt, g_q, sinks):
    s = qt.shape[1]
    blk = SWA_BLOCK
    bias = jnp.asarray(_swa_bias())
    gq_col = jnp.broadcast_to(g_q.reshape(SWA_HD, 1), (SWA_HD, blk))
    sink_rows = jnp.repeat(sinks.reshape(SWA_KVH, SWA_GROUP), blk, axis=1)

    def prev(i):
        return jnp.maximum(i - 1, 0)

    return pl.pallas_call(
        _swa_kernel,
        grid=(s // blk,),
        in_specs=[
            pl.BlockSpec((D_MODEL, blk), lambda i: (0, i)),
            pl.BlockSpec((blk, SWA_KVW), lambda i: (prev(i), 0)),
            pl.BlockSpec((blk, SWA_KVW), lambda i: (i, 0)),
            pl.BlockSpec((SWA_KVW, blk), lambda i: (0, prev(i))),
            pl.BlockSpec((SWA_KVW, blk), lambda i: (0, i)),
            pl.BlockSpec((SWA_HD, blk), lambda i: (0, 0)),
            pl.BlockSpec((SWA_KVH, SWA_GROUP * blk), lambda i: (0, 0)),
            pl.BlockSpec((1, 2 * blk, SWA_GROUP * blk), lambda i: (jnp.minimum(i, 1), 0, 0)),
        ],
        out_specs=pl.BlockSpec((D_MODEL, blk), lambda i: (0, i)),
        out_shape=jax.ShapeDtypeStruct((D_MODEL, s), BF16),
        compiler_params=_params(1),
        name="swa_core",
    )(qt, k, k, vt, vt, gq_col, sink_rows, bias)


def kernel(x, norm_mix, norm_mlp, mlp_w1, mlp_w2, a_w_in, a_w_g2, a_b_g, a_g_o, a_w_o, kv_norm,
           kv_w_k, kv_w_v, kv_g_k, b_w_q, b_g_q, b_sinks, b_w_o):
    assert x.shape == (1, SEQ, D_MODEL)
    h = x[0]
    k_shared = vt_shared = None
    a_w_in_t = jnp.swapaxes(a_w_in, 1, 2)
    for layer in range(DEPTH):
        if layer < N_GLA:
            proj, glr = _gla_inproj(h, norm_mix[layer], a_w_in_t, layer, tm=1024, tn=1024)
            o = _gla_core(proj, glr, a_w_g2[layer], a_b_g[layer], a_g_o[layer])
            h = _matmul_residual(o, a_w_o, h, tm=1024, tn=1024, layer=layer)
        else:
            j = layer - N_GLA
            if k_shared is None:
                qt, k_shared, vt_shared = _swa_proj(h, norm_mix[layer], b_w_q, j, tm=1024, tn=1024,
                                                    kv=(kv_norm, kv_w_k, kv_w_v, kv_g_k))
            else:
                (qt,) = _swa_proj(h, norm_mix[layer], b_w_q, j, tm=1024, tn=1024)
            ot = _swa_core(qt, k_shared, vt_shared, b_g_q[j], b_sinks[j])
            h = _matmul_residual(ot, b_w_o, h, tm=1024, tn=1024, layer=j, a_transposed=True)
        h = _mlp(h, norm_mlp[layer], mlp_w1, mlp_w2, tm=1024, tf=512, layer=layer)
    return h[None]
```

```python
import functools

import jax
import jax.numpy as jnp
import numpy as np
from jax import lax
from jax.experimental import pallas as pl
from jax.experimental.pallas import tpu as pltpu

F32 = jnp.float32
BF16 = jnp.bfloat16

EPS = 1e-6
LOG2_E = 1.4426950408889634
D_MODEL = 2048
SEQ = 8192
DEPTH = 4
N_GLA = DEPTH // 2
D_FF = 4 * D_MODEL

GLA_HEADS = 4
GLA_DK = 256
GLA_DV = 512
GLA_RANK = 16
GLA_TAU = 16.0
GLA_QK = GLA_HEADS * GLA_DK
GLA_VV = GLA_HEADS * GLA_DV
GLA_MAIN = 2 * GLA_QK + 2 * GLA_VV
GLA_CHUNK = 256
LANES = 128
ROWS_PER_SLAB = 256

SWA_HD = 64
SWA_QH = 32
SWA_KVH = 4
SWA_GROUP = SWA_QH // SWA_KVH
SWA_WINDOW = 128
SWA_BLOCK = 128
SWA_KVW = SWA_KVH * SWA_HD

V7X_VMEM_BYTES = 64 * 1024 * 1024
VMEM_LIMIT = V7X_VMEM_BYTES - 4 * 1024 * 1024


def _params(n_axes):
    return pltpu.CompilerParams(
        dimension_semantics=("arbitrary",) * n_axes, vmem_limit_bytes=VMEM_LIMIT)


def _rmsnorm_rows(x, g):
    ms = jnp.mean(x * x, axis=-1, keepdims=True)
    return x * lax.rsqrt(ms + EPS) * g


def _dot_nt(a, b):
    return lax.dot_general(a, b, (((1,), (1,)), ((), ())), preferred_element_type=F32)


def _dot_tn(a, b):
    return lax.dot_general(a, b, (((0,), (0,)), ((), ())), preferred_element_type=F32)


def _weight_spec(layer, block, index_map):
    if layer is None:
        return pl.BlockSpec(block, index_map)
    return pl.BlockSpec((None,) + block, lambda *ij: (layer,) + index_map(*ij))


def _row_slabs(n_rows):
    return [slice(r, r + ROWS_PER_SLAB) for r in range(0, n_rows, ROWS_PER_SLAB)]


def _dot_wt(w, hn):
    return lax.dot_general(w, hn, (((0,), (1,)), ((), ())), preferred_element_type=F32)


def _gla_inproj_kernel(x_ref, g_ref, wt_ref, wgt_ref, o_ref, glr_ref, hn_ref):
    @pl.when(pl.program_id(1) == 0)
    def _():
        w = wt_ref[...].astype(BF16)
        wg = wgt_ref[...]
        row = lax.broadcasted_iota(jnp.int32, wg.shape, 0)
        wg = jnp.where(row < GLA_RANK, wg, 0.0).astype(BF16)
        for sl in _row_slabs(x_ref.shape[0]):
            hn = _rmsnorm_rows(x_ref[sl, :], g_ref[...]).astype(BF16)
            hn_ref[sl, :] = hn
            glr_ref[sl, :] = _dot_nt(hn, wg)
            o_ref[sl, :] = _dot_nt(hn, w).astype(o_ref.dtype)

    @pl.when(pl.program_id(1) != 0)
    def _():
        o_ref[...] = _dot_nt(hn_ref[...], wt_ref[...].astype(BF16)).astype(o_ref.dtype)


def _gla_inproj(x, g, wt, layer, tm, tn):
    s, d = x.shape
    return pl.pallas_call(
        _gla_inproj_kernel,
        grid=(s // tm, GLA_MAIN // tn),
        in_specs=[
            pl.BlockSpec((tm, d), lambda i, j: (i, 0)),
            pl.BlockSpec((1, d), lambda i, j: (0, 0)),
            pl.BlockSpec((None, tn, d), lambda i, j: (layer, j, 0)),
            pl.BlockSpec((None, LANES, d), lambda i, j: (layer, GLA_MAIN // LANES, 0)),
        ],
        out_specs=[
            pl.BlockSpec((tm, tn), lambda i, j: (i, j)),
            pl.BlockSpec((tm, LANES), lambda i, j: (i, 0)),
        ],
        out_shape=[
            jax.ShapeDtypeStruct((s, GLA_MAIN), BF16),
            jax.ShapeDtypeStruct((s, LANES), F32),
        ],
        scratch_shapes=[pltpu.VMEM((tm, d), BF16)],
        compiler_params=_params(2),
        name="gla_inproj",
    )(x, g.reshape(1, d), wt, wt)


def _swa_proj_kernel(*refs, with_kv):
    if with_kv:
        (x_ref, gq_ref, wq_ref, gkv_ref, wk_ref, wv_ref, gk_ref,
         qt_ref, k_ref, vt_ref, hnq_ref) = refs
    else:
        x_ref, gq_ref, wq_ref, qt_ref, hnq_ref = refs

    @pl.when(pl.program_id(1) == 0)
    def _():
        wq = wq_ref[...].astype(BF16)
        if with_kv:
            wk = wk_ref[...].astype(BF16)
            wv = wv_ref[...].astype(BF16)
        for sl in _row_slabs(x_ref.shape[0]):
            x = x_ref[sl, :]
            y = x * lax.rsqrt(jnp.mean(x * x, axis=-1, keepdims=True) + EPS)
            hn_q = (y * gq_ref[...]).astype(BF16)
            hnq_ref[sl, :] = hn_q
            qt_ref[:, sl] = _dot_wt(wq, hn_q).astype(qt_ref.dtype)
            if with_kv:
                hn_kv = (y * gkv_ref[...]).astype(BF16)
                k = jnp.dot(hn_kv, wk, preferred_element_type=F32)
                heads = [_rmsnorm_rows(k[:, h * SWA_HD:(h + 1) * SWA_HD], gk_ref[...])
                         for h in range(SWA_KVH)]
                k_ref[sl, :] = jnp.concatenate(heads, axis=1).astype(k_ref.dtype)
                vt_ref[:, sl] = _dot_wt(wv, hn_kv).astype(vt_ref.dtype)

    @pl.when(pl.program_id(1) != 0)
    def _():
        qt_ref[...] = _dot_wt(wq_ref[...].astype(BF16), hnq_ref[...]).astype(qt_ref.dtype)


def _swa_proj(x, g_q, w_q, layer, tm, tn, kv=None):
    s, d = x.shape
    n = w_q.shape[-1]
    in_specs = [
        pl.BlockSpec((tm, d), lambda i, j: (i, 0)),
        pl.BlockSpec((1, d), lambda i, j: (0, 0)),
        pl.BlockSpec((None, d, tn), lambda i, j: (layer, 0, j)),
    ]
    args = [x, g_q.reshape(1, d), w_q]
    out_specs = [pl.BlockSpec((tn, tm), lambda i, j: (j, i))]
    out_shape = [jax.ShapeDtypeStruct((n, s), BF16)]
    scratch = [pltpu.VMEM((tm, d), BF16)]
    if kv is not None:
        g_kv, w_k, w_v, g_k = kv
        in_specs += [
            pl.BlockSpec((1, d), lambda i, j: (0, 0)),
            pl.BlockSpec((d, SWA_KVW), lambda i, j: (0, 0)),
            pl.BlockSpec((d, SWA_KVW), lambda i, j: (0, 0)),
            pl.BlockSpec((1, SWA_HD), lambda i, j: (0, 0)),
        ]
        args += [g_kv.reshape(1, d), w_k, w_v, g_k.reshape(1, SWA_HD)]
        out_specs += [
            pl.BlockSpec((tm, SWA_KVW), lambda i, j: (i, 0)),
            pl.BlockSpec((SWA_KVW, tm), lambda i, j: (0, i)),
        ]
        out_shape += [
            jax.ShapeDtypeStruct((s, SWA_KVW), BF16),
            jax.ShapeDtypeStruct((SWA_KVW, s), BF16),
        ]
    return pl.pallas_call(
        functools.partial(_swa_proj_kernel, with_kv=kv is not None),
        grid=(s // tm, n // tn),
        in_specs=in_specs,
        out_specs=out_specs,
        out_shape=out_shape,
        scratch_shapes=scratch,
        compiler_params=_params(2),
        name="swa_proj",
    )(*args)


def _matmul_residual_kernel(a_ref, w_ref, r_ref, o_ref, wb_ref, *, a_transposed):
    @pl.when(pl.program_id(1) == 0)
    def _():
        wb_ref[...] = w_ref[...].astype(BF16)

    if a_transposed:
        y = _dot_tn(a_ref[...], wb_ref[...])
    else:
        y = jnp.dot(a_ref[...], wb_ref[...], preferred_element_type=F32)
    o_ref[...] = r_ref[...] + y


def _matmul_residual(a, w, res, tm, tn, layer=None, a_transposed=False):
    k, n = w.shape[-2:]
    s = res.shape[0]
    a_spec = (pl.BlockSpec((k, tm), lambda j, i: (0, i)) if a_transposed
              else pl.BlockSpec((tm, k), lambda j, i: (i, 0)))
    return pl.pallas_call(
        functools.partial(_matmul_residual_kernel, a_transposed=a_transposed),
        grid=(n // tn, s // tm),
        in_specs=[
            a_spec,
            _weight_spec(layer, (k, tn), lambda j, i: (0, j)),
            pl.BlockSpec((tm, tn), lambda j, i: (i, j)),
        ],
        out_specs=pl.BlockSpec((tm, tn), lambda j, i: (i, j)),
        out_shape=jax.ShapeDtypeStruct((s, n), F32),
        scratch_shapes=[pltpu.VMEM((k, tn), BF16)],
        compiler_params=_params(2),
        name="matmul_residual",
    )(a, w, res)


def _mlp_kernel(x_ref, g_ref, w1_ref, w2_ref, o_ref, hn_ref):
    def hidden_out(hn, w1, w2):
        u = jnp.maximum(jnp.dot(hn, w1, preferred_element_type=F32), 0.0)
        return jnp.dot((u * u).astype(BF16), w2, preferred_element_type=F32)

    @pl.when(pl.program_id(1) == 0)
    def _():
        w1 = w1_ref[...].astype(BF16)
        w2 = w2_ref[...].astype(BF16)
        for sl in _row_slabs(x_ref.shape[0]):
            x = x_ref[sl, :]
            hn = _rmsnorm_rows(x, g_ref[...]).astype(BF16)
            hn_ref[sl, :] = hn
            o_ref[sl, :] = x + hidden_out(hn, w1, w2)

    @pl.when(pl.program_id(1) != 0)
    def _():
        o_ref[...] += hidden_out(hn_ref[...], w1_ref[...].astype(BF16), w2_ref[...].astype(BF16))


def _mlp(x, g, w1, w2, tm, tf, layer=None):
    s, d = x.shape
    f = w1.shape[-1]
    return pl.pallas_call(
        _mlp_kernel,
        grid=(s // tm, f // tf),
        in_specs=[
            pl.BlockSpec((tm, d), lambda i, j: (i, 0)),
            pl.BlockSpec((1, d), lambda i, j: (0, 0)),
            _weight_spec(layer, (d, tf), lambda i, j: (0, j)),
            _weight_spec(layer, (tf, d), lambda i, j: (j, 0)),
        ],
        out_specs=pl.BlockSpec((tm, d), lambda i, j: (i, 0)),
        out_shape=jax.ShapeDtypeStruct((s, d), F32),
        scratch_shapes=[pltpu.VMEM((tm, d), BF16)],
        compiler_params=_params(2),
        name="mlp",
    )(x, g.reshape(1, d), w1, w2)


@functools.lru_cache(maxsize=None)
def _gla_constants(c):
    i = np.arange(c)[:, None]
    t = np.arange(c)[None, :]
    mats = [t <= i, t > i]
    masks = []
    s = c // 2
    while s >= 1:
        pos = i % (2 * s)
        ref_row = i - pos + s
        upper = pos >= s
        mats.append(np.where(upper, (t > ref_row) & (t <= i), (t > i) & (t <= ref_row)))
        same_block = (i // (2 * s)) == (t // (2 * s))
        masks.append(same_block & upper & ((t % (2 * s)) < s))
        s //= 2
    masks.append(i == t)
    sums = np.concatenate(mats, axis=0).astype(np.float32)
    return sums, np.stack(masks).astype(np.float32)


def _gla_kernel(q_ref, k_ref, v_ref, r_ref, glr_ref, wg_ref, bg_ref, go_ref, sums_ref, mask_ref,
                o_ref, st_ref):
    c = GLA_CHUNK
    n_levels = mask_ref.shape[0] - 1

    @pl.when(pl.program_id(0) == 0)
    def _():
        st_ref[...] = jnp.zeros_like(st_ref)

    gate_in = jnp.dot(glr_ref[...].astype(BF16), wg_ref[...], preferred_element_type=F32)
    gate_in = gate_in + bg_ref[...]
    log_sig = jnp.minimum(gate_in, 0.0) - jnp.log1p(jnp.exp(-jnp.abs(gate_in)))
    log_a = log_sig * (LOG2_E / GLA_TAU)

    la_hi = log_a.astype(BF16)
    la_lo = (log_a - la_hi.astype(F32)).astype(BF16)
    chunk_sums = sums_ref[0:2 * c, :]
    part_chunk = (jnp.dot(chunk_sums, la_hi, preferred_element_type=F32)
                  + jnp.dot(chunk_sums, la_lo, preferred_element_type=F32))
    part_level = jnp.dot(sums_ref[2 * c:, :], la_hi, preferred_element_type=F32)

    for h in range(GLA_HEADS):
        qk_cols = slice(h * GLA_DK, (h + 1) * GLA_DK)
        v_cols = slice(h * GLA_DV, (h + 1) * GLA_DV)
        q = q_ref[:, qk_cols].astype(F32) * (GLA_DK ** -0.5)
        k = k_ref[:, qk_cols].astype(F32)
        v = v_ref[:, v_cols]

        b = part_chunk[0:c, qk_cols]
        state_t = st_ref[h]
        o = _dot_nt((q * jnp.exp2(b)).astype(BF16), state_t.astype(BF16))

        attn = _dot_nt(q.astype(BF16), k.astype(BF16)) * mask_ref[n_levels]
        for lvl in range(n_levels):
            e = jnp.exp2(part_level[lvl * c:(lvl + 1) * c, qk_cols])
            attn = attn + _dot_nt((q * e).astype(BF16), (k * e).astype(BF16)) * mask_ref[lvl]
        o = o + jnp.dot(attn.astype(BF16), v, preferred_element_type=F32)

        k_dec = (k * jnp.exp2(part_chunk[c:2 * c, qk_cols])).astype(BF16)
        st_ref[h] = state_t * jnp.exp2(b[c - 1:c, :]) + _dot_tn(v, k_dec)

        o = _rmsnorm_rows(o, go_ref[...])
        r = r_ref[:, v_cols].astype(F32)
        o_ref[:, v_cols] = (o * (r * jax.nn.sigmoid(r))).astype(o_ref.dtype)


def _gla_core(proj, glr, w_g2, b_g, g_o):
    s = proj.shape[0]
    c = GLA_CHUNK
    sums, masks = _gla_constants(c)
    sums = jnp.asarray(sums, dtype=BF16)
    masks = jnp.asarray(masks, dtype=F32)
    w_g2p = jnp.pad(w_g2, ((0, LANES - GLA_RANK), (0, 0))).astype(BF16)
    assert 2 * GLA_QK == GLA_VV
    return pl.pallas_call(
        _gla_kernel,
        grid=(s // c,),
        in_specs=[
            pl.BlockSpec((c, GLA_QK), lambda i: (i, 0)),
            pl.BlockSpec((c, GLA_QK), lambda i: (i, 1)),
            pl.BlockSpec((c, GLA_VV), lambda i: (i, 1)),
            pl.BlockSpec((c, GLA_VV), lambda i: (i, 2)),
            pl.BlockSpec((c, LANES), lambda i: (i, 0)),
            pl.BlockSpec((LANES, GLA_QK), lambda i: (0, 0)),
            pl.BlockSpec((1, GLA_QK), lambda i: (0, 0)),
            pl.BlockSpec((1, GLA_DV), lambda i: (0, 0)),
            pl.BlockSpec(sums.shape, lambda i: (0, 0)),
            pl.BlockSpec(masks.shape, lambda i: (0, 0, 0)),
        ],
        out_specs=pl.BlockSpec((c, GLA_VV), lambda i: (i, 0)),
        out_shape=jax.ShapeDtypeStruct((s, GLA_VV), BF16),
        scratch_shapes=[pltpu.VMEM((GLA_HEADS, GLA_DV, GLA_DK), F32)],
        compiler_params=_params(1),
        name="gla_core",
    )(proj, proj, proj, proj, glr, w_g2p, b_g.reshape(1, GLA_QK), g_o.reshape(1, GLA_DV),
      sums, masks)


@functools.lru_cache(maxsize=None)
def _swa_bias():
    blk = SWA_BLOCK
    kj = np.arange(2 * blk)[:, None]
    qi = np.arange(blk)[None, :]
    rel = qi + blk - kj
    valid = (rel >= 0) & (rel < SWA_WINDOW)
    first = valid & (kj >= blk)
    planes = np.stack([first, valid])
    bias = np.where(planes, 0.0, -np.inf).astype(np.float32)
    return np.tile(bias, (1, 1, SWA_GROUP))


def _swa_kernel(qt_ref, kp_ref, kc_ref, vtp_ref, vtc_ref, gq_ref, sink_ref, bias_ref, o_ref):
    hd = SWA_HD
    blk = SWA_BLOCK
    k_band = jnp.concatenate([kp_ref[...], kc_ref[...]], axis=0)
    vt_band = jnp.concatenate([vtp_ref[...], vtc_ref[...]], axis=1)
    bias = bias_ref[0]
    for kvh in range(SWA_KVH):
        cols = []
        for g in range(SWA_GROUP):
            r0 = (kvh * SWA_GROUP + g) * hd
            q_g = qt_ref[r0:r0 + hd, :].astype(F32)
            ms = jnp.mean(q_g * q_g, axis=0, keepdims=True)
            scale = lax.rsqrt(ms + EPS) * (hd ** -0.5 * LOG2_E)
            cols.append((q_g * scale * gq_ref[...]).astype(BF16))
        q_cat = jnp.concatenate(cols, axis=1)
        s_t = jnp.dot(k_band[:, kvh * hd:(kvh + 1) * hd], q_cat, preferred_element_type=F32) + bias
        sink = sink_ref[kvh:kvh + 1, :] * LOG2_E
        m = jnp.maximum(jnp.max(s_t, axis=0, keepdims=True), sink)
        e = jnp.exp2(s_t - m)
        denom = jnp.sum(e, axis=0, keepdims=True) + jnp.exp2(sink - m)
        o_t = jnp.dot(vt_band[kvh * hd:(kvh + 1) * hd, :], e.astype(BF16), preferred_element_type=F32)
        o_t = o_t * (1.0 / denom)
        for g in range(SWA_GROUP):
            r0 = (kvh * SWA_GROUP + g) * hd
            o_ref[r0:r0 + hd, :] = o_t[:, g * blk:(g + 1) * blk].astype(o_ref.dtype)


def _swa_core(qt, k, vt, g_q, sinks):
    s = qt.shape[1]
    blk = SWA_BLOCK
    bias = jnp.asarray(_swa_bias())
    gq_col = jnp.broadcast_to(g_q.reshape(SWA_HD, 1), (SWA_HD, blk))
    sink_rows = jnp.repeat(sinks.reshape(SWA_KVH, SWA_GROUP), blk, axis=1)

    def prev(i):
        return jnp.maximum(i - 1, 0)

    return pl.pallas_call(
        _swa_kernel,
        grid=(s // blk,),
        in_specs=[
            pl.BlockSpec((D_MODEL, blk), lambda i: (0, i)),
            pl.BlockSpec((blk, SWA_KVW), lambda i: (prev(i), 0)),
            pl.BlockSpec((blk, SWA_KVW), lambda i: (i, 0)),
            pl.BlockSpec((SWA_KVW, blk), lambda i: (0, prev(i))),
            pl.BlockSpec((SWA_KVW, blk), lambda i: (0, i)),
            pl.BlockSpec((SWA_HD, blk), lambda i: (0, 0)),
            pl.BlockSpec((SWA_KVH, SWA_GROUP * blk), lambda i: (0, 0)),
            pl.BlockSpec((1, 2 * blk, SWA_GROUP * blk), lambda i: (jnp.minimum(i, 1), 0, 0)),
        ],
        out_specs=pl.BlockSpec((D_MODEL, blk), lambda i: (0, i)),
        out_shape=jax.ShapeDtypeStruct((D_MODEL, s), BF16),
        compiler_params=_params(1),
        name="swa_core",
    )(qt, k, k, vt, vt, gq_col, sink_rows, bias)


def kernel(x, norm_mix, norm_mlp, mlp_w1, mlp_w2, a_w_in, a_w_g2, a_b_g, a_g_o, a_w_o, kv_norm,
           kv_w_k, kv_w_v, kv_g_k, b_w_q, b_g_q, b_sinks, b_w_o):
    assert x.shape == (1, SEQ, D_MODEL)
    h = x[0]
    k_shared = vt_shared = None
    a_w_in_t = jnp.swapaxes(a_w_in, 1, 2)
    for layer in range(DEPTH):
        if layer < N_GLA:
            proj, glr = _gla_inproj(h, norm_mix[layer], a_w_in_t, layer, tm=1024, tn=1024)
            o = _gla_core(proj, glr, a_w_g2[layer], a_b_g[layer], a_g_o[layer])
            h = _matmul_residual(o, a_w_o, h, tm=1024, tn=1024, layer=layer)
        else:
            j = layer - N_GLA
            if k_shared is None:
                qt, k_shared, vt_shared = _swa_proj(h, norm_mix[layer], b_w_q, j, tm=1024, tn=1024,
                                                    kv=(kv_norm, kv_w_k, kv_w_v, kv_g_k))
            else:
                (qt,) = _swa_proj(h, norm_mix[layer], b_w_q, j, tm=1024, tn=1024)
            ot = _swa_core(qt, k_shared, vt_shared, b_g_q[j], b_sinks[j])
            h = _matmul_residual(ot, b_w_o, h, tm=1024, tn=1024, layer=j, a_transposed=True)
        h = _mlp(h, norm_mlp[layer], mlp_w1, mlp_w2, tm=1024, tf=512, layer=layer)
    return h[None]
```

```python
import functools

import jax
import jax.numpy as jnp
import numpy as np
from jax import lax
from jax.experimental import pallas as pl
from jax.experimental.pallas import tpu as pltpu

F32 = jnp.float32
BF16 = jnp.bfloat16

EPS = 1e-6
LOG2_E = 1.4426950408889634
D_MODEL = 2048
SEQ = 8192
DEPTH = 4
N_GLA = DEPTH // 2
D_FF = 4 * D_MODEL

GLA_HEADS = 4
GLA_DK = 256
GLA_DV = 512
GLA_RANK = 16
GLA_TAU = 16.0
GLA_QK = GLA_HEADS * GLA_DK
GLA_VV = GLA_HEADS * GLA_DV
GLA_MAIN = 2 * GLA_QK + 2 * GLA_VV
GLA_CHUNK = 256
LANES = 128
ROWS_PER_SLAB = 256

SWA_HD = 64
SWA_QH = 32
SWA_KVH = 4
SWA_GROUP = SWA_QH // SWA_KVH
SWA_WINDOW = 128
SWA_BLOCK = 128
SWA_KVW = SWA_KVH * SWA_HD
SWA_BLOCKS_PER_STEP = 4

V7X_VMEM_BYTES = 64 * 1024 * 1024
VMEM_LIMIT = V7X_VMEM_BYTES - 4 * 1024 * 1024


def _params(n_axes):
    return pltpu.CompilerParams(
        dimension_semantics=("arbitrary",) * n_axes, vmem_limit_bytes=VMEM_LIMIT)


def _rmsnorm_rows(x, g):
    ms = jnp.mean(x * x, axis=-1, keepdims=True)
    return x * lax.rsqrt(ms + EPS) * g


def _dot_nt(a, b):
    return lax.dot_general(a, b, (((1,), (1,)), ((), ())), preferred_element_type=F32)


def _dot_tn(a, b):
    return lax.dot_general(a, b, (((0,), (0,)), ((), ())), preferred_element_type=F32)


def _weight_spec(layer, block, index_map):
    if layer is None:
        return pl.BlockSpec(block, index_map)
    return pl.BlockSpec((None,) + block, lambda *ij: (layer,) + index_map(*ij))


def _row_slabs(n_rows):
    return [slice(r, r + ROWS_PER_SLAB) for r in range(0, n_rows, ROWS_PER_SLAB)]


def _dot_wt(w, hn):
    return lax.dot_general(w, hn, (((0,), (1,)), ((), ())), preferred_element_type=F32)


def _gla_inproj_kernel(x_ref, g_ref, wt_ref, wgt_ref, o_ref, glr_ref, hn_ref):
    @pl.when(pl.program_id(1) == 0)
    def _():
        w = wt_ref[...].astype(BF16)
        wg = wgt_ref[...]
        row = lax.broadcasted_iota(jnp.int32, wg.shape, 0)
        wg = jnp.where(row < GLA_RANK, wg, 0.0).astype(BF16)
        for sl in _row_slabs(x_ref.shape[0]):
            hn = _rmsnorm_rows(x_ref[sl, :], g_ref[...]).astype(BF16)
            hn_ref[sl, :] = hn
            glr_ref[sl, :] = _dot_nt(hn, wg)
            o_ref[sl, :] = _dot_nt(hn, w).astype(o_ref.dtype)

    @pl.when(pl.program_id(1) != 0)
    def _():
        o_ref[...] = _dot_nt(hn_ref[...], wt_ref[...].astype(BF16)).astype(o_ref.dtype)


def _gla_inproj(x, g, wt, layer, tm, tn):
    s, d = x.shape
    return pl.pallas_call(
        _gla_inproj_kernel,
        grid=(s // tm, GLA_MAIN // tn),
        in_specs=[
            pl.BlockSpec((tm, d), lambda i, j: (i, 0)),
            pl.BlockSpec((1, d), lambda i, j: (0, 0)),
            pl.BlockSpec((None, tn, d), lambda i, j: (layer, j, 0)),
            pl.BlockSpec((None, LANES, d), lambda i, j: (layer, GLA_MAIN // LANES, 0)),
        ],
        out_specs=[
            pl.BlockSpec((tm, tn), lambda i, j: (i, j)),
            pl.BlockSpec((tm, LANES), lambda i, j: (i, 0)),
        ],
        out_shape=[
            jax.ShapeDtypeStruct((s, GLA_MAIN), BF16),
            jax.ShapeDtypeStruct((s, LANES), F32),
        ],
        scratch_shapes=[pltpu.VMEM((tm, d), BF16)],
        compiler_params=_params(2),
        name="gla_inproj",
    )(x, g.reshape(1, d), wt, wt)


def _swa_proj_kernel(*refs, with_kv):
    if with_kv:
        (x_ref, gq_ref, wq_ref, gkv_ref, wk_ref, wv_ref, gk_ref,
         qt_ref, k_ref, vt_ref, hnq_ref) = refs
    else:
        x_ref, gq_ref, wq_ref, qt_ref, hnq_ref = refs

    @pl.when(pl.program_id(1) == 0)
    def _():
        wq = wq_ref[...].astype(BF16)
        if with_kv:
            wk = wk_ref[...].astype(BF16)
            wv = wv_ref[...].astype(BF16)
        for sl in _row_slabs(x_ref.shape[0]):
            x = x_ref[sl, :]
            y = x * lax.rsqrt(jnp.mean(x * x, axis=-1, keepdims=True) + EPS)
            hn_q = (y * gq_ref[...]).astype(BF16)
            hnq_ref[sl, :] = hn_q
            qt_ref[:, sl] = _dot_wt(wq, hn_q).astype(qt_ref.dtype)
            if with_kv:
                hn_kv = (y * gkv_ref[...]).astype(BF16)
                k = jnp.dot(hn_kv, wk, preferred_element_type=F32)
                heads = [_rmsnorm_rows(k[:, h * SWA_HD:(h + 1) * SWA_HD], gk_ref[...])
                         for h in range(SWA_KVH)]
                k_ref[sl, :] = jnp.concatenate(heads, axis=1).astype(k_ref.dtype)
                vt_ref[:, sl] = _dot_wt(wv, hn_kv).astype(vt_ref.dtype)

    @pl.when(pl.program_id(1) != 0)
    def _():
        qt_ref[...] = _dot_wt(wq_ref[...].astype(BF16), hnq_ref[...]).astype(qt_ref.dtype)


def _swa_proj(x, g_q, w_q, layer, tm, tn, kv=None):
    s, d = x.shape
    n = w_q.shape[-1]
    in_specs = [
        pl.BlockSpec((tm, d), lambda i, j: (i, 0)),
        pl.BlockSpec((1, d), lambda i, j: (0, 0)),
        pl.BlockSpec((None, d, tn), lambda i, j: (layer, 0, j)),
    ]
    args = [x, g_q.reshape(1, d), w_q]
    out_specs = [pl.BlockSpec((tn, tm), lambda i, j: (j, i))]
    out_shape = [jax.ShapeDtypeStruct((n, s), BF16)]
    scratch = [pltpu.VMEM((tm, d), BF16)]
    if kv is not None:
        g_kv, w_k, w_v, g_k = kv
        in_specs += [
            pl.BlockSpec((1, d), lambda i, j: (0, 0)),
            pl.BlockSpec((d, SWA_KVW), lambda i, j: (0, 0)),
            pl.BlockSpec((d, SWA_KVW), lambda i, j: (0, 0)),
            pl.BlockSpec((1, SWA_HD), lambda i, j: (0, 0)),
        ]
        args += [g_kv.reshape(1, d), w_k, w_v, g_k.reshape(1, SWA_HD)]
        out_specs += [
            pl.BlockSpec((tm, SWA_KVW), lambda i, j: (i, 0)),
            pl.BlockSpec((SWA_KVW, tm), lambda i, j: (0, i)),
        ]
        out_shape += [
            jax.ShapeDtypeStruct((s, SWA_KVW), BF16),
            jax.ShapeDtypeStruct((SWA_KVW, s), BF16),
        ]
    return pl.pallas_call(
        functools.partial(_swa_proj_kernel, with_kv=kv is not None),
        grid=(s // tm, n // tn),
        in_specs=in_specs,
        out_specs=out_specs,
        out_shape=out_shape,
        scratch_shapes=scratch,
        compiler_params=_params(2),
        name="swa_proj",
    )(*args)


def _matmul_residual_kernel(a_ref, w_ref, r_ref, o_ref, wb_ref, *, a_transposed):
    @pl.when(pl.program_id(1) == 0)
    def _():
        wb_ref[...] = w_ref[...].astype(BF16)

    if a_transposed:
        y = _dot_tn(a_ref[...], wb_ref[...])
    else:
        y = jnp.dot(a_ref[...], wb_ref[...], preferred_element_type=F32)
    o_ref[...] = r_ref[...] + y


def _matmul_residual(a, w, res, tm, tn, layer=None, a_transposed=False):
    k, n = w.shape[-2:]
    s = res.shape[0]
    a_spec = (pl.BlockSpec((k, tm), lambda j, i: (0, i)) if a_transposed
              else pl.BlockSpec((tm, k), lambda j, i: (i, 0)))
    return pl.pallas_call(
        functools.partial(_matmul_residual_kernel, a_transposed=a_transposed),
        grid=(n // tn, s // tm),
        in_specs=[
            a_spec,
            _weight_spec(layer, (k, tn), lambda j, i: (0, j)),
            pl.BlockSpec((tm, tn), lambda j, i: (i, j)),
        ],
        out_specs=pl.BlockSpec((tm, tn), lambda j, i: (i, j)),
        out_shape=jax.ShapeDtypeStruct((s, n), F32),
        scratch_shapes=[pltpu.VMEM((k, tn), BF16)],
        compiler_params=_params(2),
        name="matmul_residual",
    )(a, w, res)


def _mlp_kernel(x_ref, g_ref, w1_ref, w2_ref, o_ref, hn_ref):
    def hidden_out(hn, w1, w2):
        u = jnp.maximum(jnp.dot(hn, w1, preferred_element_type=F32), 0.0)
        return jnp.dot((u * u).astype(BF16), w2, preferred_element_type=F32)

    @pl.when(pl.program_id(1) == 0)
    def _():
        w1 = w1_ref[...].astype(BF16)
        w2 = w2_ref[...].astype(BF16)
        for sl in _row_slabs(x_ref.shape[0]):
            x = x_ref[sl, :]
            hn = _rmsnorm_rows(x, g_ref[...]).astype(BF16)
            hn_ref[sl, :] = hn
            o_ref[sl, :] = x + hidden_out(hn, w1, w2)

    @pl.when(pl.program_id(1) != 0)
    def _():
        o_ref[...] += hidden_out(hn_ref[...], w1_ref[...].astype(BF16), w2_ref[...].astype(BF16))


def _mlp(x, g, w1, w2, tm, tf, layer=None):
    s, d = x.shape
    f = w1.shape[-1]
    return pl.pallas_call(
        _mlp_kernel,
        grid=(s // tm, f // tf),
        in_specs=[
            pl.BlockSpec((tm, d), lambda i, j: (i, 0)),
            pl.BlockSpec((1, d), lambda i, j: (0, 0)),
            _weight_spec(layer, (d, tf), lambda i, j: (0, j)),
            _weight_spec(layer, (tf, d), lambda i, j: (j, 0)),
        ],
        out_specs=pl.BlockSpec((tm, d), lambda i, j: (i, 0)),
        out_shape=jax.ShapeDtypeStruct((s, d), F32),
        scratch_shapes=[pltpu.VMEM((tm, d), BF16)],
        compiler_params=_params(2),
        name="mlp",
    )(x, g.reshape(1, d), w1, w2)


@functools.lru_cache(maxsize=None)
def _gla_constants(c):
    i = np.arange(c)[:, None]
    t = np.arange(c)[None, :]
    mats = [t <= i, t > i]
    masks = []
    s = c // 2
    while s >= 1:
        pos = i % (2 * s)
        ref_row = i - pos + s
        upper = pos >= s
        mats.append(np.where(upper, (t > ref_row) & (t <= i), (t > i) & (t <= ref_row)))
        same_block = (i // (2 * s)) == (t // (2 * s))
        masks.append(same_block & upper & ((t % (2 * s)) < s))
        s //= 2
    masks.append(i == t)
    sums = np.concatenate(mats, axis=0).astype(np.float32)
    return sums, np.stack(masks).astype(np.float32)


def _gla_kernel(q_ref, k_ref, v_ref, r_ref, glr_ref, wg_ref, bg_ref, go_ref, sums_ref, mask_ref,
                o_ref, st_ref):
    c = GLA_CHUNK
    n_levels = mask_ref.shape[0] - 1

    @pl.when(pl.program_id(0) == 0)
    def _():
        st_ref[...] = jnp.zeros_like(st_ref)

    gate_in = jnp.dot(glr_ref[...].astype(BF16), wg_ref[...], preferred_element_type=F32)
    gate_in = gate_in + bg_ref[...]
    log_sig = jnp.minimum(gate_in, 0.0) - jnp.log1p(jnp.exp(-jnp.abs(gate_in)))
    log_a = log_sig * (LOG2_E / GLA_TAU)

    la_hi = log_a.astype(BF16)
    la_lo = (log_a - la_hi.astype(F32)).astype(BF16)
    chunk_sums = sums_ref[0:2 * c, :]
    part_chunk = (jnp.dot(chunk_sums, la_hi, preferred_element_type=F32)
                  + jnp.dot(chunk_sums, la_lo, preferred_element_type=F32))
    part_level = jnp.dot(sums_ref[2 * c:, :], la_hi, preferred_element_type=F32)

    for h in range(GLA_HEADS):
        qk_cols = slice(h * GLA_DK, (h + 1) * GLA_DK)
        v_cols = slice(h * GLA_DV, (h + 1) * GLA_DV)
        q = q_ref[:, qk_cols] * (GLA_DK ** -0.5)
        k = k_ref[:, qk_cols]
        v = v_ref[:, v_cols]

        b = part_chunk[0:c, qk_cols]
        state_t = st_ref[h]
        o = _dot_nt(q * jnp.exp2(b).astype(BF16), state_t.astype(BF16))

        attn = _dot_nt(q, k) * mask_ref[n_levels]
        for lvl in range(n_levels):
            e = jnp.exp2(part_level[lvl * c:(lvl + 1) * c, qk_cols]).astype(BF16)
            attn = attn + _dot_nt(q * e, k * e) * mask_ref[lvl]
        o = o + jnp.dot(attn.astype(BF16), v, preferred_element_type=F32)

        k_dec = k * jnp.exp2(part_chunk[c:2 * c, qk_cols]).astype(BF16)
        st_ref[h] = state_t * jnp.exp2(b[c - 1:c, :]) + _dot_tn(v, k_dec)

        o = _rmsnorm_rows(o, go_ref[...])
        r = r_ref[:, v_cols].astype(F32)
        o_ref[:, v_cols] = (o * (r * jax.nn.sigmoid(r))).astype(o_ref.dtype)


def _gla_core(proj, glr, w_g2, b_g, g_o):
    s = proj.shape[0]
    c = GLA_CHUNK
    sums, masks = _gla_constants(c)
    sums = jnp.asarray(sums, dtype=BF16)
    masks = jnp.asarray(masks, dtype=F32)
    w_g2p = jnp.pad(w_g2, ((0, LANES - GLA_RANK), (0, 0))).astype(BF16)
    assert 2 * GLA_QK == GLA_VV
    return pl.pallas_call(
        _gla_kernel,
        grid=(s // c,),
        in_specs=[
            pl.BlockSpec((c, GLA_QK), lambda i: (i, 0)),
            pl.BlockSpec((c, GLA_QK), lambda i: (i, 1)),
            pl.BlockSpec((c, GLA_VV), lambda i: (i, 1)),
            pl.BlockSpec((c, GLA_VV), lambda i: (i, 2)),
            pl.BlockSpec((c, LANES), lambda i: (i, 0)),
            pl.BlockSpec((LANES, GLA_QK), lambda i: (0, 0)),
            pl.BlockSpec((1, GLA_QK), lambda i: (0, 0)),
            pl.BlockSpec((1, GLA_DV), lambda i: (0, 0)),
            pl.BlockSpec(sums.shape, lambda i: (0, 0)),
            pl.BlockSpec(masks.shape, lambda i: (0, 0, 0)),
        ],
        out_specs=pl.BlockSpec((c, GLA_VV), lambda i: (i, 0)),
        out_shape=jax.ShapeDtypeStruct((s, GLA_VV), BF16),
        scratch_shapes=[pltpu.VMEM((GLA_HEADS, GLA_DV, GLA_DK), F32)],
        compiler_params=_params(1),
        name="gla_core",
    )(proj, proj, proj, proj, glr, w_g2p, b_g.reshape(1, GLA_QK), g_o.reshape(1, GLA_DV),
      sums, masks)


@functools.lru_cache(maxsize=None)
def _swa_bias():
    blk = SWA_BLOCK
    kj = np.arange(2 * blk)[:, None]
    qi = np.arange(blk)[None, :]
    rel = qi + blk - kj
    valid = (rel >= 0) & (rel < SWA_WINDOW)
    first = valid & (kj >= blk)
    planes = np.stack([first, valid])
    bias = np.where(planes, 0.0, -np.inf).astype(np.float32)
    return np.tile(bias, (1, 1, SWA_GROUP))


def _swa_kernel(qt_ref, kp_ref, kc_ref, vtp_ref, vtc_ref, gq_ref, sink_ref, bias_ref, o_ref):
    hd = SWA_HD
    blk = SWA_BLOCK
    k_all = jnp.concatenate([kp_ref[...], kc_ref[...]], axis=0)
    vt_all = jnp.concatenate([vtp_ref[...], vtc_ref[...]], axis=1)
    for sub in range(SWA_BLOCKS_PER_STEP):
        q_cols = slice(sub * blk, (sub + 1) * blk)
        k_band = k_all[sub * blk:(sub + 2) * blk, :]
        vt_band = vt_all[:, sub * blk:(sub + 2) * blk]
        if sub == 0:
            bias = bias_ref[jnp.minimum(pl.program_id(0), 1)]
        else:
            bias = bias_ref[1]
        for kvh in range(SWA_KVH):
            cols = []
            for g in range(SWA_GROUP):
                r0 = (kvh * SWA_GROUP + g) * hd
                q_g = qt_ref[r0:r0 + hd, q_cols].astype(F32)
                ms = jnp.mean(q_g * q_g, axis=0, keepdims=True)
                scale = lax.rsqrt(ms + EPS) * (hd ** -0.5 * LOG2_E)
                cols.append((q_g * scale * gq_ref[...]).astype(BF16))
            q_cat = jnp.concatenate(cols, axis=1)
            s_t = jnp.dot(k_band[:, kvh * hd:(kvh + 1) * hd], q_cat,
                          preferred_element_type=F32) + bias
            sink = sink_ref[kvh:kvh + 1, :] * LOG2_E
            m = jnp.maximum(jnp.max(s_t, axis=0, keepdims=True), sink)
            e = jnp.exp2(s_t - m)
            denom = jnp.sum(e, axis=0, keepdims=True) + jnp.exp2(sink - m)
            o_t = jnp.dot(vt_band[kvh * hd:(kvh + 1) * hd, :], e.astype(BF16),
                          preferred_element_type=F32)
            o_t = o_t * (1.0 / denom)
            for g in range(SWA_GROUP):
                r0 = (kvh * SWA_GROUP + g) * hd
                o_ref[r0:r0 + hd, q_cols] = o_t[:, g * blk:(g + 1) * blk].astype(o_ref.dtype)


def _swa_core(qt, k, vt, g_q, sinks):
    s = qt.shape[1]
    blk = SWA_BLOCK
    n = SWA_BLOCKS_PER_STEP
    bias = jnp.asarray(_swa_bias())
    gq_col = jnp.broadcast_to(g_q.reshape(SWA_HD, 1), (SWA_HD, blk))
    sink_rows = jnp.repeat(sinks.reshape(SWA_KVH, SWA_GROUP), blk, axis=1)

    def prev(i):
        return jnp.maximum(i * n - 1, 0)

    return pl.pallas_call(
        _swa_kernel,
        grid=(s // (n * blk),),
        in_specs=[
            pl.BlockSpec((D_MODEL, n * blk), lambda i: (0, i)),
            pl.BlockSpec((blk, SWA_KVW), lambda i: (prev(i), 0)),
            pl.BlockSpec((n * blk, SWA_KVW), lambda i: (i, 0)),
            pl.BlockSpec((SWA_KVW, blk), lambda i: (0, prev(i))),
            pl.BlockSpec((SWA_KVW, n * blk), lambda i: (0, i)),
            pl.BlockSpec((SWA_HD, blk), lambda i: (0, 0)),
            pl.BlockSpec((SWA_KVH, SWA_GROUP * blk), lambda i: (0, 0)),
            pl.BlockSpec(bias.shape, lambda i: (0, 0, 0)),
        ],
        out_specs=pl.BlockSpec((D_MODEL, n * blk), lambda i: (0, i)),
        out_shape=jax.ShapeDtypeStruct((D_MODEL, s), BF16),
        compiler_params=_params(1),
        name="swa_core",
    )(qt, k, k, vt, vt, gq_col, sink_rows, bias)


def kernel(x, norm_mix, norm_mlp, mlp_w1, mlp_w2, a_w_in, a_w_g2, a_b_g, a_g_o, a_w_o, kv_norm,
           kv_w_k, kv_w_v, kv_g_k, b_w_q, b_g_q, b_sinks, b_w_o):
    assert x.shape == (1, SEQ, D_MODEL)
    h = x[0]
    k_shared = vt_shared = None
    a_w_in_t = jnp.swapaxes(a_w_in, 1, 2)
    for layer in range(DEPTH):
        if layer < N_GLA:
            proj, glr = _gla_inproj(h, norm_mix[layer], a_w_in_t, layer, tm=1024, tn=1024)
            o = _gla_core(proj, glr, a_w_g2[layer], a_b_g[layer], a_g_o[layer])
            h = _matmul_residual(o, a_w_o, h, tm=1024, tn=1024, layer=layer)
        else:
            j = layer - N_GLA
            if k_shared is None:
                qt, k_shared, vt_shared = _swa_proj(h, norm_mix[layer], b_w_q, j, tm=1024, tn=1024,
                                                    kv=(kv_norm, kv_w_k, kv_w_v, kv_g_k))
            else:
                (qt,) = _swa_proj(h, norm_mix[layer], b_w_q, j, tm=1024, tn=1024)
            ot = _swa_core(qt, k_shared, vt_shared, b_g_q[j], b_sinks[j])
            h = _matmul_residual(ot, b_w_o, h, tm=1024, tn=1024, layer=j, a_transposed=True)
        h = _mlp(h, norm_mlp[layer], mlp_w1, mlp_w2, tm=1024, tf=512, layer=layer)
    return h[None]
```

```python
import functools

import jax
import jax.numpy as jnp
import numpy as np
from jax import lax
from jax.experimental import pallas as pl
from jax.experimental.pallas import tpu as pltpu

F32 = jnp.float32
BF16 = jnp.bfloat16

EPS = 1e-6
LOG2_E = 1.4426950408889634
D_MODEL = 2048
SEQ = 8192
DEPTH = 4
N_GLA = DEPTH // 2
D_FF = 4 * D_MODEL

GLA_HEADS = 4
GLA_DK = 256
GLA_DV = 512
GLA_RANK = 16
GLA_TAU = 16.0
GLA_QK = GLA_HEADS * GLA_DK
GLA_VV = GLA_HEADS * GLA_DV
GLA_MAIN = 2 * GLA_QK + 2 * GLA_VV
GLA_CHUNK = 256
LANES = 128
ROWS_PER_SLAB = 256

SWA_HD = 64
SWA_QH = 32
SWA_KVH = 4
SWA_GROUP = SWA_QH // SWA_KVH
SWA_WINDOW = 128
SWA_BLOCK = 128
SWA_KVW = SWA_KVH * SWA_HD
SWA_BLOCKS_PER_STEP = 4

V7X_VMEM_BYTES = 64 * 1024 * 1024
VMEM_LIMIT = V7X_VMEM_BYTES - 2 * 1024 * 1024


def _params(n_axes):
    return pltpu.CompilerParams(
        dimension_semantics=("arbitrary",) * n_axes, vmem_limit_bytes=VMEM_LIMIT)


def _rmsnorm_rows(x, g):
    ms = jnp.mean(x * x, axis=-1, keepdims=True)
    return x * lax.rsqrt(ms + EPS) * g


def _dot_nt(a, b):
    return lax.dot_general(a, b, (((1,), (1,)), ((), ())), preferred_element_type=F32)


def _dot_tn(a, b):
    return lax.dot_general(a, b, (((0,), (0,)), ((), ())), preferred_element_type=F32)


def _weight_spec(layer, block, index_map):
    if layer is None:
        return pl.BlockSpec(block, index_map)
    return pl.BlockSpec((None,) + block, lambda *ij: (layer,) + index_map(*ij))


def _row_slabs(n_rows):
    return [slice(r, r + ROWS_PER_SLAB) for r in range(0, n_rows, ROWS_PER_SLAB)]


def _dot_wt(w, hn):
    return lax.dot_general(w, hn, (((0,), (1,)), ((), ())), preferred_element_type=F32)


def _gla_inproj_kernel(x_ref, g_ref, wt_ref, wgt_ref, o_ref, glr_ref, hn_ref):
    @pl.when(pl.program_id(1) == 0)
    def _():
        w = wt_ref[...].astype(BF16)
        wg = wgt_ref[...]
        row = lax.broadcasted_iota(jnp.int32, wg.shape, 0)
        wg = jnp.where(row < GLA_RANK, wg, 0.0).astype(BF16)
        for sl in _row_slabs(x_ref.shape[0]):
            hn = _rmsnorm_rows(x_ref[sl, :], g_ref[...]).astype(BF16)
            hn_ref[sl, :] = hn
            glr_ref[sl, :] = _dot_nt(hn, wg)
            o_ref[sl, :] = _dot_nt(hn, w).astype(o_ref.dtype)

    @pl.when(pl.program_id(1) != 0)
    def _():
        o_ref[...] = _dot_nt(hn_ref[...], wt_ref[...].astype(BF16)).astype(o_ref.dtype)


def _gla_inproj(x, g, wt, layer, tm, tn):
    s, d = x.shape
    return pl.pallas_call(
        _gla_inproj_kernel,
        grid=(s // tm, GLA_MAIN // tn),
        in_specs=[
            pl.BlockSpec((tm, d), lambda i, j: (i, 0)),
            pl.BlockSpec((1, d), lambda i, j: (0, 0)),
            pl.BlockSpec((None, tn, d), lambda i, j: (layer, j, 0)),
            pl.BlockSpec((None, LANES, d), lambda i, j: (layer, GLA_MAIN // LANES, 0)),
        ],
        out_specs=[
            pl.BlockSpec((tm, tn), lambda i, j: (i, j)),
            pl.BlockSpec((tm, LANES), lambda i, j: (i, 0)),
        ],
        out_shape=[
            jax.ShapeDtypeStruct((s, GLA_MAIN), BF16),
            jax.ShapeDtypeStruct((s, LANES), F32),
        ],
        scratch_shapes=[pltpu.VMEM((tm, d), BF16)],
        compiler_params=_params(2),
        name="gla_inproj",
    )(x, g.reshape(1, d), wt, wt)


def _swa_proj_kernel(*refs, with_kv):
    if with_kv:
        (x_ref, gq_ref, wq_ref, gkv_ref, wk_ref, wv_ref, gk_ref,
         qt_ref, k_ref, vt_ref, hnq_ref) = refs
    else:
        x_ref, gq_ref, wq_ref, qt_ref, hnq_ref = refs

    @pl.when(pl.program_id(1) == 0)
    def _():
        wq = wq_ref[...].astype(BF16)
        if with_kv:
            wk = wk_ref[...].astype(BF16)
            wv = wv_ref[...].astype(BF16)
        for sl in _row_slabs(x_ref.shape[0]):
            x = x_ref[sl, :]
            y = x * lax.rsqrt(jnp.mean(x * x, axis=-1, keepdims=True) + EPS)
            hn_q = (y * gq_ref[...]).astype(BF16)
            hnq_ref[sl, :] = hn_q
            qt_ref[:, sl] = _dot_wt(wq, hn_q).astype(qt_ref.dtype)
            if with_kv:
                hn_kv = (y * gkv_ref[...]).astype(BF16)
                k = jnp.dot(hn_kv, wk, preferred_element_type=F32)
                heads = [_rmsnorm_rows(k[:, h * SWA_HD:(h + 1) * SWA_HD], gk_ref[...])
                         for h in range(SWA_KVH)]
                k_ref[sl, :] = jnp.concatenate(heads, axis=1).astype(k_ref.dtype)
                vt_ref[:, sl] = _dot_wt(wv, hn_kv).astype(vt_ref.dtype)

    @pl.when(pl.program_id(1) != 0)
    def _():
        qt_ref[...] = _dot_wt(wq_ref[...].astype(BF16), hnq_ref[...]).astype(qt_ref.dtype)


def _swa_proj(x, g_q, w_q, layer, tm, tn, kv=None):
    s, d = x.shape
    n = w_q.shape[-1]
    in_specs = [
        pl.BlockSpec((tm, d), lambda i, j: (i, 0)),
        pl.BlockSpec((1, d), lambda i, j: (0, 0)),
        pl.BlockSpec((None, d, tn), lambda i, j: (layer, 0, j)),
    ]
    args = [x, g_q.reshape(1, d), w_q]
    out_specs = [pl.BlockSpec((tn, tm), lambda i, j: (j, i))]
    out_shape = [jax.ShapeDtypeStruct((n, s), BF16)]
    scratch = [pltpu.VMEM((tm, d), BF16)]
    if kv is not None:
        g_kv, w_k, w_v, g_k = kv
        in_specs += [
            pl.BlockSpec((1, d), lambda i, j: (0, 0)),
            pl.BlockSpec((d, SWA_KVW), lambda i, j: (0, 0)),
            pl.BlockSpec((d, SWA_KVW), lambda i, j: (0, 0)),
            pl.BlockSpec((1, SWA_HD), lambda i, j: (0, 0)),
        ]
        args += [g_kv.reshape(1, d), w_k, w_v, g_k.reshape(1, SWA_HD)]
        out_specs += [
            pl.BlockSpec((tm, SWA_KVW), lambda i, j: (i, 0)),
            pl.BlockSpec((SWA_KVW, tm), lambda i, j: (0, i)),
        ]
        out_shape += [
            jax.ShapeDtypeStruct((s, SWA_KVW), BF16),
            jax.ShapeDtypeStruct((SWA_KVW, s), BF16),
        ]
    return pl.pallas_call(
        functools.partial(_swa_proj_kernel, with_kv=kv is not None),
        grid=(s // tm, n // tn),
        in_specs=in_specs,
        out_specs=out_specs,
        out_shape=out_shape,
        scratch_shapes=scratch,
        compiler_params=_params(2),
        name="swa_proj",
    )(*args)


def _matmul_residual_kernel(a_ref, w_ref, r_ref, o_ref, wb_ref, *, a_transposed):
    @pl.when(pl.program_id(1) == 0)
    def _():
        wb_ref[...] = w_ref[...].astype(BF16)

    if a_transposed:
        y = _dot_tn(a_ref[...], wb_ref[...])
    else:
        y = jnp.dot(a_ref[...], wb_ref[...], preferred_element_type=F32)
    o_ref[...] = r_ref[...] + y


def _matmul_residual(a, w, res, tm, tn, layer=None, a_transposed=False):
    k, n = w.shape[-2:]
    s = res.shape[0]
    a_spec = (pl.BlockSpec((k, tm), lambda j, i: (0, i)) if a_transposed
              else pl.BlockSpec((tm, k), lambda j, i: (i, 0)))
    return pl.pallas_call(
        functools.partial(_matmul_residual_kernel, a_transposed=a_transposed),
        grid=(n // tn, s // tm),
        in_specs=[
            a_spec,
            _weight_spec(layer, (k, tn), lambda j, i: (0, j)),
            pl.BlockSpec((tm, tn), lambda j, i: (i, j)),
        ],
        out_specs=pl.BlockSpec((tm, tn), lambda j, i: (i, j)),
        out_shape=jax.ShapeDtypeStruct((s, n), F32),
        scratch_shapes=[pltpu.VMEM((k, tn), BF16)],
        compiler_params=_params(2),
        name="matmul_residual",
    )(a, w, res)


def _mlp_kernel(x_ref, g_ref, w1_ref, w2_ref, o_ref, hn_ref):
    def hidden_out(hn, w1, w2):
        u = jnp.maximum(jnp.dot(hn, w1, preferred_element_type=F32), 0.0)
        return jnp.dot((u * u).astype(BF16), w2, preferred_element_type=F32)

    @pl.when(pl.program_id(1) == 0)
    def _():
        w1 = w1_ref[...].astype(BF16)
        w2 = w2_ref[...].astype(BF16)
        for sl in _row_slabs(x_ref.shape[0]):
            x = x_ref[sl, :]
            hn = _rmsnorm_rows(x, g_ref[...]).astype(BF16)
            hn_ref[sl, :] = hn
            o_ref[sl, :] = x + hidden_out(hn, w1, w2)

    @pl.when(pl.program_id(1) != 0)
    def _():
        o_ref[...] += hidden_out(hn_ref[...], w1_ref[...].astype(BF16), w2_ref[...].astype(BF16))


def _mlp(x, g, w1, w2, tm, tf, layer=None):
    s, d = x.shape
    f = w1.shape[-1]
    return pl.pallas_call(
        _mlp_kernel,
        grid=(s // tm, f // tf),
        in_specs=[
            pl.BlockSpec((tm, d), lambda i, j: (i, 0)),
            pl.BlockSpec((1, d), lambda i, j: (0, 0)),
            _weight_spec(layer, (d, tf), lambda i, j: (0, j)),
            _weight_spec(layer, (tf, d), lambda i, j: (j, 0)),
        ],
        out_specs=pl.BlockSpec((tm, d), lambda i, j: (i, 0)),
        out_shape=jax.ShapeDtypeStruct((s, d), F32),
        scratch_shapes=[pltpu.VMEM((tm, d), BF16)],
        compiler_params=_params(2),
        name="mlp",
    )(x, g.reshape(1, d), w1, w2)


def _mlp_weight_slab_specs(layer, n_steps):
    tf = D_FF // n_steps
    in_specs = [
        pl.BlockSpec((None, D_MODEL, tf), lambda i: (layer, 0, i)),
        pl.BlockSpec((None, tf, D_MODEL), lambda i: (layer, i, 0)),
    ]
    out_specs = [
        pl.BlockSpec((D_MODEL, tf), lambda i: (0, i)),
        pl.BlockSpec((tf, D_MODEL), lambda i: (i, 0)),
    ]
    out_shape = [
        jax.ShapeDtypeStruct((D_MODEL, D_FF), BF16),
        jax.ShapeDtypeStruct((D_FF, D_MODEL), BF16),
    ]
    return in_specs, out_specs, out_shape


def _round_mlp_weight_slabs(w1_ref, w2_ref, w1b_ref, w2b_ref):
    w1b_ref[...] = w1_ref[...].astype(BF16)
    w2b_ref[...] = w2_ref[...].astype(BF16)


@functools.lru_cache(maxsize=None)
def _gla_constants(c):
    i = np.arange(c)[:, None]
    t = np.arange(c)[None, :]
    mats = [t <= i, t > i]
    masks = []
    s = c // 2
    while s >= 1:
        pos = i % (2 * s)
        ref_row = i - pos + s
        upper = pos >= s
        mats.append(np.where(upper, (t > ref_row) & (t <= i), (t > i) & (t <= ref_row)))
        same_block = (i // (2 * s)) == (t // (2 * s))
        masks.append(same_block & upper & ((t % (2 * s)) < s))
        s //= 2
    masks.append(i == t)
    sums = np.concatenate(mats, axis=0).astype(np.float32)
    return sums, np.stack(masks).astype(np.float32)


def _gla_kernel(q_ref, k_ref, v_ref, r_ref, glr_ref, wg_ref, bg_ref, go_ref, sums_ref, mask_ref,
                w1_ref, w2_ref, o_ref, w1b_ref, w2b_ref, st_ref):
    c = GLA_CHUNK
    _round_mlp_weight_slabs(w1_ref, w2_ref, w1b_ref, w2b_ref)
    n_levels = mask_ref.shape[0] - 1

    @pl.when(pl.program_id(0) == 0)
    def _():
        st_ref[...] = jnp.zeros_like(st_ref)

    gate_in = jnp.dot(glr_ref[...].astype(BF16), wg_ref[...], preferred_element_type=F32)
    gate_in = gate_in + bg_ref[...]
    log_sig = jnp.minimum(gate_in, 0.0) - jnp.log1p(jnp.exp(-jnp.abs(gate_in)))
    log_a = log_sig * (LOG2_E / GLA_TAU)

    la_hi = log_a.astype(BF16)
    la_lo = (log_a - la_hi.astype(F32)).astype(BF16)
    chunk_sums = sums_ref[0:2 * c, :]
    part_chunk = (jnp.dot(chunk_sums, la_hi, preferred_element_type=F32)
                  + jnp.dot(chunk_sums, la_lo, preferred_element_type=F32))
    part_level = jnp.dot(sums_ref[2 * c:, :], la_hi, preferred_element_type=F32)

    for h in range(GLA_HEADS):
        qk_cols = slice(h * GLA_DK, (h + 1) * GLA_DK)
        v_cols = slice(h * GLA_DV, (h + 1) * GLA_DV)
        q = q_ref[:, qk_cols] * (GLA_DK ** -0.5)
        k = k_ref[:, qk_cols]
        v = v_ref[:, v_cols]

        b = part_chunk[0:c, qk_cols]
        state_t = st_ref[h]
        o = _dot_nt(q * jnp.exp2(b).astype(BF16), state_t.astype(BF16))

        attn = _dot_nt(q, k) * mask_ref[n_levels]
        for lvl in range(n_levels):
            e = jnp.exp2(part_level[lvl * c:(lvl + 1) * c, qk_cols]).astype(BF16)
            attn = attn + _dot_nt(q * e, k * e) * mask_ref[lvl]
        o = o + jnp.dot(attn.astype(BF16), v, preferred_element_type=F32)

        k_dec = k * jnp.exp2(part_chunk[c:2 * c, qk_cols]).astype(BF16)
        st_ref[h] = state_t * jnp.exp2(b[c - 1:c, :]) + _dot_tn(v, k_dec)

        o = _rmsnorm_rows(o, go_ref[...])
        r = r_ref[:, v_cols].astype(F32)
        o_ref[:, v_cols] = (o * (r * jax.nn.sigmoid(r))).astype(o_ref.dtype)


def _gla_core(proj, glr, w_g2, b_g, g_o, mlp_w1, mlp_w2, layer):
    s = proj.shape[0]
    c = GLA_CHUNK
    w_in_specs, w_out_specs, w_out_shape = _mlp_weight_slab_specs(layer, s // c)
    sums, masks = _gla_constants(c)
    sums = jnp.asarray(sums, dtype=BF16)
    masks = jnp.asarray(masks, dtype=F32)
    w_g2p = jnp.pad(w_g2, ((0, LANES - GLA_RANK), (0, 0))).astype(BF16)
    assert 2 * GLA_QK == GLA_VV
    return pl.pallas_call(
        _gla_kernel,
        grid=(s // c,),
        in_specs=[
            pl.BlockSpec((c, GLA_QK), lambda i: (i, 0)),
            pl.BlockSpec((c, GLA_QK), lambda i: (i, 1)),
            pl.BlockSpec((c, GLA_VV), lambda i: (i, 1)),
            pl.BlockSpec((c, GLA_VV), lambda i: (i, 2)),
            pl.BlockSpec((c, LANES), lambda i: (i, 0)),
            pl.BlockSpec((LANES, GLA_QK), lambda i: (0, 0)),
            pl.BlockSpec((1, GLA_QK), lambda i: (0, 0)),
            pl.BlockSpec((1, GLA_DV), lambda i: (0, 0)),
            pl.BlockSpec(sums.shape, lambda i: (0, 0)),
            pl.BlockSpec(masks.shape, lambda i: (0, 0, 0)),
        ] + w_in_specs,
        out_specs=[pl.BlockSpec((c, GLA_VV), lambda i: (i, 0))] + w_out_specs,
        out_shape=[jax.ShapeDtypeStruct((s, GLA_VV), BF16)] + w_out_shape,
        scratch_shapes=[pltpu.VMEM((GLA_HEADS, GLA_DV, GLA_DK), F32)],
        compiler_params=_params(1),
        name="gla_core",
    )(proj, proj, proj, proj, glr, w_g2p, b_g.reshape(1, GLA_QK), g_o.reshape(1, GLA_DV),
      sums, masks, mlp_w1, mlp_w2)


@functools.lru_cache(maxsize=None)
def _swa_bias():
    blk = SWA_BLOCK
    kj = np.arange(2 * blk)[:, None]
    qi = np.arange(blk)[None, :]
    rel = qi + blk - kj
    valid = (rel >= 0) & (rel < SWA_WINDOW)
    first = valid & (kj >= blk)
    planes = np.stack([first, valid])
    bias = np.where(planes, 0.0, -np.inf).astype(np.float32)
    return np.tile(bias, (1, 1, SWA_GROUP))


def _swa_kernel(qt_ref, kp_ref, kc_ref, vtp_ref, vtc_ref, gq_ref, sink_ref, bias_ref,
                w1_ref, w2_ref, o_ref, w1b_ref, w2b_ref):
    _round_mlp_weight_slabs(w1_ref, w2_ref, w1b_ref, w2b_ref)
    hd = SWA_HD
    blk = SWA_BLOCK
    k_all = jnp.concatenate([kp_ref[...], kc_ref[...]], axis=0)
    vt_all = jnp.concatenate([vtp_ref[...], vtc_ref[...]], axis=1)
    for sub in range(SWA_BLOCKS_PER_STEP):
        q_cols = slice(sub * blk, (sub + 1) * blk)
        k_band = k_all[sub * blk:(sub + 2) * blk, :]
        vt_band = vt_all[:, sub * blk:(sub + 2) * blk]
        if sub == 0:
            bias = bias_ref[jnp.minimum(pl.program_id(0), 1)]
        else:
            bias = bias_ref[1]
        for kvh in range(SWA_KVH):
            cols = []
            for g in range(SWA_GROUP):
                r0 = (kvh * SWA_GROUP + g) * hd
                q_g = qt_ref[r0:r0 + hd, q_cols].astype(F32)
                ms = jnp.mean(q_g * q_g, axis=0, keepdims=True)
                scale = lax.rsqrt(ms + EPS) * (hd ** -0.5 * LOG2_E)
                cols.append((q_g * scale * gq_ref[...]).astype(BF16))
            q_cat = jnp.concatenate(cols, axis=1)
            s_t = jnp.dot(k_band[:, kvh * hd:(kvh + 1) * hd], q_cat,
                          preferred_element_type=F32) + bias
            sink = sink_ref[kvh:kvh + 1, :] * LOG2_E
            m = jnp.maximum(jnp.max(s_t, axis=0, keepdims=True), sink)
            e = jnp.exp2(s_t - m)
            denom = jnp.sum(e, axis=0, keepdims=True) + jnp.exp2(sink - m)
            o_t = jnp.dot(vt_band[kvh * hd:(kvh + 1) * hd, :], e.astype(BF16),
                          preferred_element_type=F32)
            o_t = o_t * (1.0 / denom)
            for g in range(SWA_GROUP):
                r0 = (kvh * SWA_GROUP + g) * hd
                o_ref[r0:r0 + hd, q_cols] = o_t[:, g * blk:(g + 1) * blk].astype(o_ref.dtype)


def _swa_core(qt, k, vt, g_q, sinks, mlp_w1, mlp_w2, layer):
    s = qt.shape[1]
    blk = SWA_BLOCK
    n = SWA_BLOCKS_PER_STEP
    w_in_specs, w_out_specs, w_out_shape = _mlp_weight_slab_specs(layer, s // (n * blk))
    bias = jnp.asarray(_swa_bias())
    gq_col = jnp.broadcast_to(g_q.reshape(SWA_HD, 1), (SWA_HD, blk))
    sink_rows = jnp.repeat(sinks.reshape(SWA_KVH, SWA_GROUP), blk, axis=1)

    def prev(i):
        return jnp.maximum(i * n - 1, 0)

    return pl.pallas_call(
        _swa_kernel,
        grid=(s // (n * blk),),
        in_specs=[
            pl.BlockSpec((D_MODEL, n * blk), lambda i: (0, i)),
            pl.BlockSpec((blk, SWA_KVW), lambda i: (prev(i), 0)),
            pl.BlockSpec((n * blk, SWA_KVW), lambda i: (i, 0)),
            pl.BlockSpec((SWA_KVW, blk), lambda i: (0, prev(i))),
            pl.BlockSpec((SWA_KVW, n * blk), lambda i: (0, i)),
            pl.BlockSpec((SWA_HD, blk), lambda i: (0, 0)),
            pl.BlockSpec((SWA_KVH, SWA_GROUP * blk), lambda i: (0, 0)),
            pl.BlockSpec(bias.shape, lambda i: (0, 0, 0)),
        ] + w_in_specs,
        out_specs=[pl.BlockSpec((D_MODEL, n * blk), lambda i: (0, i))] + w_out_specs,
        out_shape=[jax.ShapeDtypeStruct((D_MODEL, s), BF16)] + w_out_shape,
        compiler_params=_params(1),
        name="swa_core",
    )(qt, k, k, vt, vt, gq_col, sink_rows, bias, mlp_w1, mlp_w2)


def kernel(x, norm_mix, norm_mlp, mlp_w1, mlp_w2, a_w_in, a_w_g2, a_b_g, a_g_o, a_w_o, kv_norm,
           kv_w_k, kv_w_v, kv_g_k, b_w_q, b_g_q, b_sinks, b_w_o):
    assert x.shape == (1, SEQ, D_MODEL)
    h = x[0]
    k_shared = vt_shared = None
    a_w_in_t = jnp.swapaxes(a_w_in, 1, 2)
    for layer in range(DEPTH):
        if layer < N_GLA:
            proj, glr = _gla_inproj(h, norm_mix[layer], a_w_in_t, layer, tm=1024, tn=1024)
            o, w1b, w2b = _gla_core(proj, glr, a_w_g2[layer], a_b_g[layer], a_g_o[layer],
                                    mlp_w1, mlp_w2, layer)
            h = _matmul_residual(o, a_w_o, h, tm=1024, tn=1024, layer=layer)
        else:
            j = layer - N_GLA
            if k_shared is None:
                qt, k_shared, vt_shared = _swa_proj(h, norm_mix[layer], b_w_q, j, tm=1024, tn=1024,
                                                    kv=(kv_norm, kv_w_k, kv_w_v, kv_g_k))
            else:
                (qt,) = _swa_proj(h, norm_mix[layer], b_w_q, j, tm=1024, tn=1024)
            ot, w1b, w2b = _swa_core(qt, k_shared, vt_shared, b_g_q[j], b_sinks[j],
                                     mlp_w1, mlp_w2, layer)
            h = _matmul_residual(ot, b_w_o, h, tm=1024, tn=1024, layer=j, a_transposed=True)
        h = _mlp(h, norm_mlp[layer], w1b, w2b, tm=1024, tf=1024)
    return h[None]
```

```python
import functools

import jax
import jax.numpy as jnp
import numpy as np
from jax import lax
from jax.experimental import pallas as pl
from jax.experimental.pallas import tpu as pltpu

F32 = jnp.float32
BF16 = jnp.bfloat16

EPS = 1e-6
LOG2_E = 1.4426950408889634
D_MODEL = 2048
SEQ = 8192
DEPTH = 4
N_GLA = DEPTH // 2
D_FF = 4 * D_MODEL

GLA_HEADS = 4
GLA_DK = 256
GLA_DV = 512
GLA_RANK = 16
GLA_TAU = 16.0
GLA_QK = GLA_HEADS * GLA_DK
GLA_VV = GLA_HEADS * GLA_DV
GLA_MAIN = 2 * GLA_QK + 2 * GLA_VV
GLA_CHUNK = 256
LANES = 128
BF16_ROWS_PER_VREG = 16
ROWS_PER_SLAB = 256

SWA_HD = 64
SWA_QH = 32
SWA_KVH = 4
SWA_GROUP = SWA_QH // SWA_KVH
SWA_WINDOW = 128
SWA_BLOCK = 128
SWA_KVW = SWA_KVH * SWA_HD
SWA_BLOCKS_PER_STEP = 4

V7X_VMEM_BYTES = 64 * 1024 * 1024
VMEM_LIMIT = V7X_VMEM_BYTES - 2 * 1024 * 1024


def _params(n_axes):
    return pltpu.CompilerParams(
        dimension_semantics=("arbitrary",) * n_axes, vmem_limit_bytes=VMEM_LIMIT)


def _rmsnorm_rows(x, g):
    ms = jnp.mean(x * x, axis=-1, keepdims=True)
    return x * lax.rsqrt(ms + EPS) * g


def _dot_nt(a, b):
    return lax.dot_general(a, b, (((1,), (1,)), ((), ())), preferred_element_type=F32)


def _dot_tn(a, b):
    return lax.dot_general(a, b, (((0,), (0,)), ((), ())), preferred_element_type=F32)


def _weight_spec(layer, block, index_map):
    if layer is None:
        return pl.BlockSpec(block, index_map)
    return pl.BlockSpec((None,) + block, lambda *ij: (layer,) + index_map(*ij))


def _row_slabs(n_rows):
    return [slice(r, r + ROWS_PER_SLAB) for r in range(0, n_rows, ROWS_PER_SLAB)]


def _dot_wt(w, hn):
    return lax.dot_general(w, hn, (((0,), (1,)), ((), ())), preferred_element_type=F32)


def _gla_inproj_kernel(x_ref, g_ref, wt_ref, wgt_ref, o_ref, glr_ref, hn_ref, *, first_gate_tile):
    @pl.when(pl.program_id(1) == 0)
    def _():
        w = wt_ref[...].astype(BF16)
        wg = wgt_ref[...]
        row = lax.broadcasted_iota(jnp.int32, wg.shape, 0)
        wg = jnp.where(row < GLA_RANK, wg, 0.0).astype(BF16)
        for sl in _row_slabs(x_ref.shape[0]):
            hn = _rmsnorm_rows(x_ref[sl, :], g_ref[...]).astype(BF16)
            hn_ref[sl, :] = hn
            glr_ref[sl, :] = _dot_nt(hn, wg)
            o_ref[sl, :] = _dot_nt(hn, w).astype(o_ref.dtype)

    is_gate_tile = pl.program_id(1) >= first_gate_tile

    @pl.when((pl.program_id(1) != 0) & jnp.logical_not(is_gate_tile))
    def _():
        y = _dot_nt(hn_ref[...], wt_ref[...].astype(BF16))
        o_ref[...] = y.astype(o_ref.dtype)

    @pl.when(is_gate_tile)
    def _():
        y = _dot_nt(hn_ref[...], wt_ref[...].astype(BF16))
        o_ref[...] = (y * jax.nn.sigmoid(y)).astype(o_ref.dtype)


def _gla_inproj(x, g, wt, layer, tm, tn):
    s, d = x.shape
    first_gate_tile = (2 * GLA_QK + GLA_VV) // tn
    assert first_gate_tile >= 1 and (2 * GLA_QK + GLA_VV) % tn == 0
    return pl.pallas_call(
        functools.partial(_gla_inproj_kernel, first_gate_tile=first_gate_tile),
        grid=(s // tm, GLA_MAIN // tn),
        in_specs=[
            pl.BlockSpec((tm, d), lambda i, j: (i, 0)),
            pl.BlockSpec((1, d), lambda i, j: (0, 0)),
            pl.BlockSpec((None, tn, d), lambda i, j: (layer, j, 0)),
            pl.BlockSpec((None, LANES, d), lambda i, j: (layer, GLA_MAIN // LANES, 0)),
        ],
        out_specs=[
            pl.BlockSpec((tm, tn), lambda i, j: (i, j)),
            pl.BlockSpec((tm, LANES), lambda i, j: (i, 0)),
        ],
        out_shape=[
            jax.ShapeDtypeStruct((s, GLA_MAIN), BF16),
            jax.ShapeDtypeStruct((s, LANES), F32),
        ],
        scratch_shapes=[pltpu.VMEM((tm, d), BF16)],
        compiler_params=_params(2),
        name="gla_inproj",
    )(x, g.reshape(1, d), wt, wt)


def _swa_proj_kernel(*refs, with_kv):
    if with_kv:
        (x_ref, gq_ref, wq_ref, ghq_ref, gkv_ref, wk_ref, wv_ref, gk_ref,
         qt_ref, k_ref, vt_ref, hnq_ref) = refs
    else:
        x_ref, gq_ref, wq_ref, ghq_ref, qt_ref, hnq_ref = refs

    def head_normed(q_t):
        heads = []
        for r0 in range(0, q_t.shape[0], SWA_HD):
            q_h = q_t[r0:r0 + SWA_HD, :]
            ms = jnp.mean(q_h * q_h, axis=0, keepdims=True)
            scale = lax.rsqrt(ms + EPS) * (SWA_HD ** -0.5 * LOG2_E)
            heads.append(q_h * scale * ghq_ref[...])
        return jnp.concatenate(heads, axis=0).astype(qt_ref.dtype)

    @pl.when(pl.program_id(1) == 0)
    def _():
        wq = wq_ref[...].astype(BF16)
        if with_kv:
            wk = wk_ref[...].astype(BF16)
            wv = wv_ref[...].astype(BF16)
        for sl in _row_slabs(x_ref.shape[0]):
            x = x_ref[sl, :]
            y = x * lax.rsqrt(jnp.mean(x * x, axis=-1, keepdims=True) + EPS)
            hn_q = (y * gq_ref[...]).astype(BF16)
            hnq_ref[sl, :] = hn_q
            qt_ref[:, sl] = head_normed(_dot_wt(wq, hn_q))
            if with_kv:
                hn_kv = (y * gkv_ref[...]).astype(BF16)
                k = jnp.dot(hn_kv, wk, preferred_element_type=F32)
                heads = [_rmsnorm_rows(k[:, h * SWA_HD:(h + 1) * SWA_HD], gk_ref[...])
                         for h in range(SWA_KVH)]
                k_ref[sl, :] = jnp.concatenate(heads, axis=1).astype(k_ref.dtype)
                vt_ref[:, sl] = _dot_wt(wv, hn_kv).astype(vt_ref.dtype)

    @pl.when(pl.program_id(1) != 0)
    def _():
        qt_ref[...] = head_normed(_dot_wt(wq_ref[...].astype(BF16), hnq_ref[...]))


def _swa_proj(x, g_q, w_q, g_hq, layer, tm, tn, kv=None):
    s, d = x.shape
    n = w_q.shape[-1]
    in_specs = [
        pl.BlockSpec((tm, d), lambda i, j: (i, 0)),
        pl.BlockSpec((1, d), lambda i, j: (0, 0)),
        _weight_spec(layer, (d, tn), lambda i, j: (0, j)),
        pl.BlockSpec((SWA_HD, 1), lambda i, j: (0, 0)),
    ]
    args = [x, g_q.reshape(1, d), w_q, g_hq.reshape(SWA_HD, 1)]
    out_specs = [pl.BlockSpec((tn, tm), lambda i, j: (j, i))]
    out_shape = [jax.ShapeDtypeStruct((n, s), BF16)]
    scratch = [pltpu.VMEM((tm, d), BF16)]
    if kv is not None:
        g_kv, w_k, w_v, g_k = kv
        in_specs += [
            pl.BlockSpec((1, d), lambda i, j: (0, 0)),
            pl.BlockSpec((d, SWA_KVW), lambda i, j: (0, 0)),
            pl.BlockSpec((d, SWA_KVW), lambda i, j: (0, 0)),
            pl.BlockSpec((1, SWA_HD), lambda i, j: (0, 0)),
        ]
        args += [g_kv.reshape(1, d), w_k, w_v, g_k.reshape(1, SWA_HD)]
        out_specs += [
            pl.BlockSpec((tm, SWA_KVW), lambda i, j: (i, 0)),
            pl.BlockSpec((SWA_KVW, tm), lambda i, j: (0, i)),
        ]
        out_shape += [
            jax.ShapeDtypeStruct((s, SWA_KVW), BF16),
            jax.ShapeDtypeStruct((SWA_KVW, s), BF16),
        ]
    return pl.pallas_call(
        functools.partial(_swa_proj_kernel, with_kv=kv is not None),
        grid=(s // tm, n // tn),
        in_specs=in_specs,
        out_specs=out_specs,
        out_shape=out_shape,
        scratch_shapes=scratch,
        compiler_params=_params(2),
        name="swa_proj",
    )(*args)


def _matmul_residual_kernel(a_ref, w_ref, r_ref, o_ref, *, a_transposed):
    if a_transposed:
        y = _dot_tn(a_ref[...], w_ref[...])
    else:
        y = jnp.dot(a_ref[...], w_ref[...], preferred_element_type=F32)
    o_ref[...] = r_ref[...] + y


def _matmul_residual(a, w, res, tm, a_transposed=False):
    k, n = w.shape
    s = res.shape[0]
    a_spec = (pl.BlockSpec((k, tm), lambda i: (0, i)) if a_transposed
              else pl.BlockSpec((tm, k), lambda i: (i, 0)))
    return pl.pallas_call(
        functools.partial(_matmul_residual_kernel, a_transposed=a_transposed),
        grid=(s // tm,),
        in_specs=[
            a_spec,
            pl.BlockSpec((k, n), lambda i: (0, 0)),
            pl.BlockSpec((tm, n), lambda i: (i, 0)),
        ],
        out_specs=pl.BlockSpec((tm, n), lambda i: (i, 0)),
        out_shape=jax.ShapeDtypeStruct((s, n), F32),
        compiler_params=_params(1),
        name="matmul_residual",
    )(a, w, res)


def _mlp_kernel(x_ref, g_ref, w1_ref, w2_ref, o_ref, hn_ref):
    def hidden_out(hn, w1, w2):
        u = jnp.maximum(jnp.dot(hn, w1, preferred_element_type=F32), 0.0)
        return jnp.dot((u * u).astype(BF16), w2, preferred_element_type=F32)

    @pl.when(pl.program_id(1) == 0)
    def _():
        w1 = w1_ref[...].astype(BF16)
        w2 = w2_ref[...].astype(BF16)
        for sl in _row_slabs(x_ref.shape[0]):
            x = x_ref[sl, :]
            hn = _rmsnorm_rows(x, g_ref[...]).astype(BF16)
            hn_ref[sl, :] = hn
            o_ref[sl, :] = x + hidden_out(hn, w1, w2)

    @pl.when(pl.program_id(1) != 0)
    def _():
        o_ref[...] += hidden_out(hn_ref[...], w1_ref[...].astype(BF16), w2_ref[...].astype(BF16))


def _mlp(x, g, w1, w2, tm, tf, layer=None):
    s, d = x.shape
    f = w1.shape[-1]
    return pl.pallas_call(
        _mlp_kernel,
        grid=(s // tm, f // tf),
        in_specs=[
            pl.BlockSpec((tm, d), lambda i, j: (i, 0)),
            pl.BlockSpec((1, d), lambda i, j: (0, 0)),
            _weight_spec(layer, (d, tf), lambda i, j: (0, j)),
            _weight_spec(layer, (tf, d), lambda i, j: (j, 0)),
        ],
        out_specs=pl.BlockSpec((tm, d), lambda i, j: (i, 0)),
        out_shape=jax.ShapeDtypeStruct((s, d), F32),
        scratch_shapes=[pltpu.VMEM((tm, d), BF16)],
        compiler_params=_params(2),
        name="mlp",
    )(x, g.reshape(1, d), w1, w2)


def _rounding_specs(jobs, n_steps):
    in_specs, out_specs, out_shape = [], [], []
    for w, layer in jobs:
        _, rows, cols = w.shape
        slab = rows // n_steps
        assert rows % n_steps == 0 and slab % BF16_ROWS_PER_VREG == 0
        in_specs.append(pl.BlockSpec((None, slab, cols), lambda i, layer=layer: (layer, i, 0)))
        out_specs.append(pl.BlockSpec((slab, cols), lambda i: (i, 0)))
        out_shape.append(jax.ShapeDtypeStruct((rows, cols), BF16))
    return in_specs, out_specs, out_shape


def _round_slabs(src_refs, dst_refs):
    for src, dst in zip(src_refs, dst_refs):
        dst[...] = src[...].astype(BF16)


@functools.lru_cache(maxsize=None)
def _gla_constants(c):
    i = np.arange(c)[:, None]
    t = np.arange(c)[None, :]
    mats = [t <= i, t > i]
    masks = []
    s = c // 2
    while s >= 1:
        pos = i % (2 * s)
        ref_row = i - pos + s
        upper = pos >= s
        mats.append(np.where(upper, (t > ref_row) & (t <= i), (t > i) & (t <= ref_row)))
        same_block = (i // (2 * s)) == (t // (2 * s))
        masks.append(same_block & upper & ((t % (2 * s)) < s))
        s //= 2
    masks.append(i == t)
    sums = np.concatenate(mats, axis=0).astype(np.float32)
    return sums, np.stack(masks).astype(np.float32)


def _gla_kernel(q_ref, k_ref, v_ref, r_ref, glr_ref, wg_ref, bg_ref, go_ref, sums_ref, mask_ref,
                *refs, n_jobs):
    o_ref, st_ref = refs[n_jobs], refs[-1]
    _round_slabs(refs[:n_jobs], refs[n_jobs + 1:-1])
    c = GLA_CHUNK
    n_levels = mask_ref.shape[0] - 1

    @pl.when(pl.program_id(0) == 0)
    def _():
        st_ref[...] = jnp.zeros_like(st_ref)

    gate_in = jnp.dot(glr_ref[...].astype(BF16), wg_ref[...], preferred_element_type=F32)
    gate_in = gate_in + bg_ref[...]
    log_sig = jnp.minimum(gate_in, 0.0) - jnp.log1p(jnp.exp(-jnp.abs(gate_in)))
    log_a = log_sig * (LOG2_E / GLA_TAU)

    la_hi = log_a.astype(BF16)
    la_lo = (log_a - la_hi.astype(F32)).astype(BF16)
    chunk_sums = sums_ref[0:2 * c, :]
    part_chunk = (jnp.dot(chunk_sums, la_hi, preferred_element_type=F32)
                  + jnp.dot(chunk_sums, la_lo, preferred_element_type=F32))
    part_level = jnp.dot(sums_ref[2 * c:, :], la_hi, preferred_element_type=F32)

    for h in range(GLA_HEADS):
        qk_cols = slice(h * GLA_DK, (h + 1) * GLA_DK)
        v_cols = slice(h * GLA_DV, (h + 1) * GLA_DV)
        q = q_ref[:, qk_cols] * (GLA_DK ** -0.5)
        k = k_ref[:, qk_cols]
        v = v_ref[:, v_cols]

        b = part_chunk[0:c, qk_cols]
        state_t = st_ref[h]
        o = _dot_nt(q * jnp.exp2(b).astype(BF16), state_t.astype(BF16))

        attn = _dot_nt(q, k) * mask_ref[n_levels]
        for lvl in range(n_levels):
            e = jnp.exp2(part_level[lvl * c:(lvl + 1) * c, qk_cols]).astype(BF16)
            attn = attn + _dot_nt(q * e, k * e) * mask_ref[lvl]
        o = o + jnp.dot(attn.astype(BF16), v, preferred_element_type=F32)

        k_dec = k * jnp.exp2(part_chunk[c:2 * c, qk_cols]).astype(BF16)
        st_ref[h] = state_t * jnp.exp2(b[c - 1:c, :]) + _dot_tn(v, k_dec)

        o = _rmsnorm_rows(o, go_ref[...])
        o_ref[:, v_cols] = (o * r_ref[:, v_cols].astype(F32)).astype(o_ref.dtype)


def _gla_core(proj, glr, w_g2, b_g, g_o, rounding_jobs):
    s = proj.shape[0]
    c = GLA_CHUNK
    w_in_specs, w_out_specs, w_out_shape = _rounding_specs(rounding_jobs, s // c)
    sums, masks = _gla_constants(c)
    sums = jnp.asarray(sums, dtype=BF16)
    masks = jnp.asarray(masks, dtype=F32)
    w_g2p = jnp.pad(w_g2, ((0, LANES - GLA_RANK), (0, 0))).astype(BF16)
    assert 2 * GLA_QK == GLA_VV
    return pl.pallas_call(
        functools.partial(_gla_kernel, n_jobs=len(rounding_jobs)),
        grid=(s // c,),
        in_specs=[
            pl.BlockSpec((c, GLA_QK), lambda i: (i, 0)),
            pl.BlockSpec((c, GLA_QK), lambda i: (i, 1)),
            pl.BlockSpec((c, GLA_VV), lambda i: (i, 1)),
            pl.BlockSpec((c, GLA_VV), lambda i: (i, 2)),
            pl.BlockSpec((c, LANES), lambda i: (i, 0)),
            pl.BlockSpec((LANES, GLA_QK), lambda i: (0, 0)),
            pl.BlockSpec((1, GLA_QK), lambda i: (0, 0)),
            pl.BlockSpec((1, GLA_DV), lambda i: (0, 0)),
            pl.BlockSpec(sums.shape, lambda i: (0, 0)),
            pl.BlockSpec(masks.shape, lambda i: (0, 0, 0)),
        ] + w_in_specs,
        out_specs=[pl.BlockSpec((c, GLA_VV), lambda i: (i, 0))] + w_out_specs,
        out_shape=[jax.ShapeDtypeStruct((s, GLA_VV), BF16)] + w_out_shape,
        scratch_shapes=[pltpu.VMEM((GLA_HEADS, GLA_DV, GLA_DK), F32)],
        compiler_params=_params(1),
        name="gla_core",
    )(proj, proj, proj, proj, glr, w_g2p, b_g.reshape(1, GLA_QK), g_o.reshape(1, GLA_DV),
      sums, masks, *[w for w, _ in rounding_jobs])


@functools.lru_cache(maxsize=None)
def _swa_bias():
    blk = SWA_BLOCK
    kj = np.arange(2 * blk)[:, None]
    qi = np.arange(blk)[None, :]
    rel = qi + blk - kj
    valid = (rel >= 0) & (rel < SWA_WINDOW)
    first = valid & (kj >= blk)
    planes = np.stack([first, valid])
    bias = np.where(planes, 0.0, -np.inf).astype(np.float32)
    return np.tile(bias, (1, 1, SWA_GROUP))


def _swa_kernel(qt_ref, kp_ref, kc_ref, vtp_ref, vtc_ref, sink_ref, bias_ref, *refs, n_jobs):
    o_ref = refs[n_jobs]
    _round_slabs(refs[:n_jobs], refs[n_jobs + 1:])
    hd = SWA_HD
    blk = SWA_BLOCK
    k_all = jnp.concatenate([kp_ref[...], kc_ref[...]], axis=0)
    vt_all = jnp.concatenate([vtp_ref[...], vtc_ref[...]], axis=1)
    for sub in range(SWA_BLOCKS_PER_STEP):
        q_cols = slice(sub * blk, (sub + 1) * blk)
        k_band = k_all[sub * blk:(sub + 2) * blk, :]
        vt_band = vt_all[:, sub * blk:(sub + 2) * blk]
        if sub == 0:
            bias = bias_ref[jnp.minimum(pl.program_id(0), 1)]
        else:
            bias = bias_ref[1]
        for kvh in range(SWA_KVH):
            cols = [qt_ref[(kvh * SWA_GROUP + g) * hd:(kvh * SWA_GROUP + g + 1) * hd, q_cols]
                    for g in range(SWA_GROUP)]
            q_cat = jnp.concatenate(cols, axis=1)
            s_t = jnp.dot(k_band[:, kvh * hd:(kvh + 1) * hd], q_cat,
                          preferred_element_type=F32) + bias
            sink = sink_ref[kvh:kvh + 1, :] * LOG2_E
            m = jnp.maximum(jnp.max(s_t, axis=0, keepdims=True), sink)
            e = jnp.exp2(s_t - m)
            denom = jnp.sum(e, axis=0, keepdims=True) + jnp.exp2(sink - m)
            o_t = jnp.dot(vt_band[kvh * hd:(kvh + 1) * hd, :], e.astype(BF16),
                          preferred_element_type=F32)
            o_t = o_t * (1.0 / denom)
            for g in range(SWA_GROUP):
                r0 = (kvh * SWA_GROUP + g) * hd
                o_ref[r0:r0 + hd, q_cols] = o_t[:, g * blk:(g + 1) * blk].astype(o_ref.dtype)


def _swa_core(qt, k, vt, sinks, rounding_jobs):
    s = qt.shape[1]
    blk = SWA_BLOCK
    n = SWA_BLOCKS_PER_STEP
    w_in_specs, w_out_specs, w_out_shape = _rounding_specs(rounding_jobs, s // (n * blk))
    bias = jnp.asarray(_swa_bias())
    sink_rows = jnp.repeat(sinks.reshape(SWA_KVH, SWA_GROUP), blk, axis=1)

    def prev(i):
        return jnp.maximum(i * n - 1, 0)

    return pl.pallas_call(
        functools.partial(_swa_kernel, n_jobs=len(rounding_jobs)),
        grid=(s // (n * blk),),
        in_specs=[
            pl.BlockSpec((D_MODEL, n * blk), lambda i: (0, i)),
            pl.BlockSpec((blk, SWA_KVW), lambda i: (prev(i), 0)),
            pl.BlockSpec((n * blk, SWA_KVW), lambda i: (i, 0)),
            pl.BlockSpec((SWA_KVW, blk), lambda i: (0, prev(i))),
            pl.BlockSpec((SWA_KVW, n * blk), lambda i: (0, i)),
            pl.BlockSpec((SWA_KVH, SWA_GROUP * blk), lambda i: (0, 0)),
            pl.BlockSpec(bias.shape, lambda i: (0, 0, 0)),
        ] + w_in_specs,
        out_specs=[pl.BlockSpec((D_MODEL, n * blk), lambda i: (0, i))] + w_out_specs,
        out_shape=[jax.ShapeDtypeStruct((D_MODEL, s), BF16)] + w_out_shape,
        compiler_params=_params(1),
        name="swa_core",
    )(qt, k, k, vt, vt, sink_rows, bias, *[w for w, _ in rounding_jobs])


def kernel(x, norm_mix, norm_mlp, mlp_w1, mlp_w2, a_w_in, a_w_g2, a_b_g, a_g_o, a_w_o, kv_norm,
           kv_w_k, kv_w_v, kv_g_k, b_w_q, b_g_q, b_sinks, b_w_o):
    assert x.shape == (1, SEQ, D_MODEL)
    h = x[0]
    k_shared = vt_shared = None
    wq_next = None
    a_w_in_t = jnp.swapaxes(a_w_in, 1, 2)
    for layer in range(DEPTH):
        jobs = [(mlp_w1, layer), (mlp_w2, layer)]
        if layer < N_GLA:
            jobs.append((a_w_o, layer))
            if layer + 1 == N_GLA:
                jobs.append((b_w_q, 0))
            proj, glr = _gla_inproj(h, norm_mix[layer], a_w_in_t, layer, tm=1024, tn=1024)
            o, w1b, w2b, wob, *rest = _gla_core(proj, glr, a_w_g2[layer], a_b_g[layer],
                                                a_g_o[layer], jobs)
            h = _matmul_residual(o, wob, h, tm=512)
        else:
            j = layer - N_GLA
            jobs.append((b_w_o, j))
            if layer + 1 < DEPTH:
                jobs.append((b_w_q, j + 1))
            kv = (kv_norm, kv_w_k, kv_w_v, kv_g_k) if k_shared is None else None
            qt, *shared = _swa_proj(h, norm_mix[layer], wq_next, b_g_q[j], None, tm=1024, tn=1024,
                                    kv=kv)
            if kv is not None:
                k_shared, vt_shared = shared
            ot, w1b, w2b, wob, *rest = _swa_core(qt, k_shared, vt_shared, b_sinks[j], jobs)
            h = _matmul_residual(ot, wob, h, tm=512, a_transposed=True)
        if rest:
            (wq_next,) = rest
        h = _mlp(h, norm_mlp[layer], w1b, w2b, tm=1024, tf=1024)
    return h[None]
```

```python
import functools

import jax
import jax.numpy as jnp
import numpy as np
from jax import lax
from jax.experimental import pallas as pl
from jax.experimental.pallas import tpu as pltpu

F32 = jnp.float32
BF16 = jnp.bfloat16

EPS = 1e-6
LOG2_E = 1.4426950408889634
D_MODEL = 2048
SEQ = 8192
DEPTH = 4
N_GLA = DEPTH // 2
D_FF = 4 * D_MODEL

GLA_HEADS = 4
GLA_DK = 256
GLA_DV = 512
GLA_RANK = 16
GLA_TAU = 16.0
GLA_QK = GLA_HEADS * GLA_DK
GLA_VV = GLA_HEADS * GLA_DV
GLA_MAIN = 2 * GLA_QK + 2 * GLA_VV
GLA_CHUNK = 256
LANES = 128
BF16_ROWS_PER_VREG = 16
ROWS_PER_SLAB = 256

SWA_HD = 64
SWA_QH = 32
SWA_KVH = 4
SWA_GROUP = SWA_QH // SWA_KVH
SWA_WINDOW = 128
SWA_BLOCK = 128
SWA_KVW = SWA_KVH * SWA_HD
SWA_BLOCKS_PER_STEP = 4

V7X_VMEM_BYTES = 64 * 1024 * 1024
VMEM_LIMIT = V7X_VMEM_BYTES - 2 * 1024 * 1024


def _params(n_axes):
    return pltpu.CompilerParams(
        dimension_semantics=("arbitrary",) * n_axes, vmem_limit_bytes=VMEM_LIMIT)


def _rmsnorm_rows(x, g):
    ms = jnp.mean(x * x, axis=-1, keepdims=True)
    return x * lax.rsqrt(ms + EPS) * g


def _dot_nt(a, b):
    return lax.dot_general(a, b, (((1,), (1,)), ((), ())), preferred_element_type=F32)


def _dot_tn(a, b):
    return lax.dot_general(a, b, (((0,), (0,)), ((), ())), preferred_element_type=F32)


def _weight_spec(layer, block, index_map):
    if layer is None:
        return pl.BlockSpec(block, index_map)
    return pl.BlockSpec((None,) + block, lambda *ij: (layer,) + index_map(*ij))


def _row_slabs(n_rows):
    return [slice(r, r + ROWS_PER_SLAB) for r in range(0, n_rows, ROWS_PER_SLAB)]


def _dot_wt(w, hn):
    return lax.dot_general(w, hn, (((0,), (1,)), ((), ())), preferred_element_type=F32)


def _gla_inproj_kernel(x_ref, g_ref, wt_ref, wgt_ref, o_ref, glr_ref, hn_ref, *, first_gate_tile):
    @pl.when(pl.program_id(1) == 0)
    def _():
        w = wt_ref[...].astype(BF16)
        wg = wgt_ref[...]
        row = lax.broadcasted_iota(jnp.int32, wg.shape, 0)
        wg = jnp.where(row < GLA_RANK, wg, 0.0).astype(BF16)
        for sl in _row_slabs(x_ref.shape[0]):
            hn = _rmsnorm_rows(x_ref[sl, :], g_ref[...]).astype(BF16)
            hn_ref[sl, :] = hn
            glr_ref[sl, :] = _dot_nt(hn, wg)
            o_ref[sl, :] = _dot_nt(hn, w).astype(o_ref.dtype)

    is_gate_tile = pl.program_id(1) >= first_gate_tile

    @pl.when((pl.program_id(1) != 0) & jnp.logical_not(is_gate_tile))
    def _():
        y = _dot_nt(hn_ref[...], wt_ref[...].astype(BF16))
        o_ref[...] = y.astype(o_ref.dtype)

    @pl.when(is_gate_tile)
    def _():
        y = _dot_nt(hn_ref[...], wt_ref[...].astype(BF16))
        o_ref[...] = (y * jax.nn.sigmoid(y)).astype(o_ref.dtype)


def _gla_inproj(x, g, wt, layer, tm, tn, w_main=None):
    s, d = x.shape
    main_spec = (pl.BlockSpec((None, tn, d), lambda i, j: (layer, j, 0)) if w_main is None
                 else pl.BlockSpec((tn, d), lambda i, j: (j, 0)))
    first_gate_tile = (2 * GLA_QK + GLA_VV) // tn
    assert first_gate_tile >= 1 and (2 * GLA_QK + GLA_VV) % tn == 0
    return pl.pallas_call(
        functools.partial(_gla_inproj_kernel, first_gate_tile=first_gate_tile),
        grid=(s // tm, GLA_MAIN // tn),
        in_specs=[
            pl.BlockSpec((tm, d), lambda i, j: (i, 0)),
            pl.BlockSpec((1, d), lambda i, j: (0, 0)),
            main_spec,
            pl.BlockSpec((None, LANES, d), lambda i, j: (layer, GLA_MAIN // LANES, 0)),
        ],
        out_specs=[
            pl.BlockSpec((tm, tn), lambda i, j: (i, j)),
            pl.BlockSpec((tm, LANES), lambda i, j: (i, 0)),
        ],
        out_shape=[
            jax.ShapeDtypeStruct((s, GLA_MAIN), BF16),
            jax.ShapeDtypeStruct((s, LANES), F32),
        ],
        scratch_shapes=[pltpu.VMEM((tm, d), BF16)],
        compiler_params=_params(2),
        name="gla_inproj",
    )(x, g.reshape(1, d), wt if w_main is None else w_main, wt)


def _swa_proj_kernel(*refs, with_kv):
    if with_kv:
        (x_ref, gq_ref, wq_ref, ghq_ref, gkv_ref, wk_ref, wv_ref, gk_ref,
         qt_ref, k_ref, vt_ref, hnq_ref) = refs
    else:
        x_ref, gq_ref, wq_ref, ghq_ref, qt_ref, hnq_ref = refs

    def head_normed(q_t):
        heads = []
        for r0 in range(0, q_t.shape[0], SWA_HD):
            q_h = q_t[r0:r0 + SWA_HD, :]
            ms = jnp.mean(q_h * q_h, axis=0, keepdims=True)
            scale = lax.rsqrt(ms + EPS) * (SWA_HD ** -0.5 * LOG2_E)
            heads.append(q_h * scale * ghq_ref[...])
        return jnp.concatenate(heads, axis=0).astype(qt_ref.dtype)

    @pl.when(pl.program_id(1) == 0)
    def _():
        wq = wq_ref[...].astype(BF16)
        if with_kv:
            wk = wk_ref[...].astype(BF16)
            wv = wv_ref[...].astype(BF16)
        for sl in _row_slabs(x_ref.shape[0]):
            x = x_ref[sl, :]
            y = x * lax.rsqrt(jnp.mean(x * x, axis=-1, keepdims=True) + EPS)
            hn_q = (y * gq_ref[...]).astype(BF16)
            hnq_ref[sl, :] = hn_q
            qt_ref[:, sl] = head_normed(_dot_wt(wq, hn_q))
            if with_kv:
                hn_kv = (y * gkv_ref[...]).astype(BF16)
                k = jnp.dot(hn_kv, wk, preferred_element_type=F32)
                heads = [_rmsnorm_rows(k[:, h * SWA_HD:(h + 1) * SWA_HD], gk_ref[...])
                         for h in range(SWA_KVH)]
                k_ref[sl, :] = jnp.concatenate(heads, axis=1).astype(k_ref.dtype)
                vt_ref[:, sl] = _dot_wt(wv, hn_kv).astype(vt_ref.dtype)

    @pl.when(pl.program_id(1) != 0)
    def _():
        qt_ref[...] = head_normed(_dot_wt(wq_ref[...].astype(BF16), hnq_ref[...]))


def _swa_proj(x, g_q, w_q, g_hq, layer, tm, tn, kv=None):
    s, d = x.shape
    n = w_q.shape[-1]
    in_specs = [
        pl.BlockSpec((tm, d), lambda i, j: (i, 0)),
        pl.BlockSpec((1, d), lambda i, j: (0, 0)),
        _weight_spec(layer, (d, tn), lambda i, j: (0, j)),
        pl.BlockSpec((SWA_HD, 1), lambda i, j: (0, 0)),
    ]
    args = [x, g_q.reshape(1, d), w_q, g_hq.reshape(SWA_HD, 1)]
    out_specs = [pl.BlockSpec((tn, tm), lambda i, j: (j, i))]
    out_shape = [jax.ShapeDtypeStruct((n, s), BF16)]
    scratch = [pltpu.VMEM((tm, d), BF16)]
    if kv is not None:
        g_kv, w_k, w_v, g_k = kv
        in_specs += [
            pl.BlockSpec((1, d), lambda i, j: (0, 0)),
            pl.BlockSpec((d, SWA_KVW), lambda i, j: (0, 0)),
            pl.BlockSpec((d, SWA_KVW), lambda i, j: (0, 0)),
            pl.BlockSpec((1, SWA_HD), lambda i, j: (0, 0)),
        ]
        args += [g_kv.reshape(1, d), w_k, w_v, g_k.reshape(1, SWA_HD)]
        out_specs += [
            pl.BlockSpec((tm, SWA_KVW), lambda i, j: (i, 0)),
            pl.BlockSpec((SWA_KVW, tm), lambda i, j: (0, i)),
        ]
        out_shape += [
            jax.ShapeDtypeStruct((s, SWA_KVW), BF16),
            jax.ShapeDtypeStruct((SWA_KVW, s), BF16),
        ]
    return pl.pallas_call(
        functools.partial(_swa_proj_kernel, with_kv=kv is not None),
        grid=(s // tm, n // tn),
        in_specs=in_specs,
        out_specs=out_specs,
        out_shape=out_shape,
        scratch_shapes=scratch,
        compiler_params=_params(2),
        name="swa_proj",
    )(*args)


def _matmul_residual_kernel(a_ref, w_ref, r_ref, o_ref, *, a_transposed):
    if a_transposed:
        y = _dot_tn(a_ref[...], w_ref[...])
    else:
        y = jnp.dot(a_ref[...], w_ref[...], preferred_element_type=F32)
    o_ref[...] = r_ref[...] + y


def _matmul_residual(a, w, res, tm, a_transposed=False):
    k, n = w.shape
    s = res.shape[0]
    a_spec = (pl.BlockSpec((k, tm), lambda i: (0, i)) if a_transposed
              else pl.BlockSpec((tm, k), lambda i: (i, 0)))
    return pl.pallas_call(
        functools.partial(_matmul_residual_kernel, a_transposed=a_transposed),
        grid=(s // tm,),
        in_specs=[
            a_spec,
            pl.BlockSpec((k, n), lambda i: (0, 0)),
            pl.BlockSpec((tm, n), lambda i: (i, 0)),
        ],
        out_specs=pl.BlockSpec((tm, n), lambda i: (i, 0)),
        out_shape=jax.ShapeDtypeStruct((s, n), F32),
        compiler_params=_params(1),
        name="matmul_residual",
    )(a, w, res)


def _mlp_kernel(x_ref, g_ref, w1_ref, w2_ref, o_ref, hn_ref):
    def hidden_out(hn, w1, w2):
        u = jnp.maximum(jnp.dot(hn, w1, preferred_element_type=F32), 0.0)
        return jnp.dot((u * u).astype(BF16), w2, preferred_element_type=F32)

    @pl.when(pl.program_id(1) == 0)
    def _():
        w1 = w1_ref[...].astype(BF16)
        w2 = w2_ref[...].astype(BF16)
        for sl in _row_slabs(x_ref.shape[0]):
            x = x_ref[sl, :]
            hn = _rmsnorm_rows(x, g_ref[...]).astype(BF16)
            hn_ref[sl, :] = hn
            o_ref[sl, :] = x + hidden_out(hn, w1, w2)

    @pl.when(pl.program_id(1) != 0)
    def _():
        o_ref[...] += hidden_out(hn_ref[...], w1_ref[...].astype(BF16), w2_ref[...].astype(BF16))


def _mlp(x, g, w1, w2, tm, tf, layer=None):
    s, d = x.shape
    f = w1.shape[-1]
    return pl.pallas_call(
        _mlp_kernel,
        grid=(s // tm, f // tf),
        in_specs=[
            pl.BlockSpec((tm, d), lambda i, j: (i, 0)),
            pl.BlockSpec((1, d), lambda i, j: (0, 0)),
            _weight_spec(layer, (d, tf), lambda i, j: (0, j)),
            _weight_spec(layer, (tf, d), lambda i, j: (j, 0)),
        ],
        out_specs=pl.BlockSpec((tm, d), lambda i, j: (i, 0)),
        out_shape=jax.ShapeDtypeStruct((s, d), F32),
        scratch_shapes=[pltpu.VMEM((tm, d), BF16)],
        compiler_params=_params(2),
        name="mlp",
    )(x, g.reshape(1, d), w1, w2)


def _rounding_specs(jobs, n_steps):
    in_specs, out_specs, out_shape = [], [], []
    for w, layer, *limit in jobs:
        _, rows, cols = w.shape
        if limit:
            (rows,) = limit
        slab = rows // n_steps
        assert rows % n_steps == 0 and slab % BF16_ROWS_PER_VREG == 0
        in_specs.append(pl.BlockSpec((None, slab, cols), lambda i, layer=layer: (layer, i, 0)))
        out_specs.append(pl.BlockSpec((slab, cols), lambda i: (i, 0)))
        out_shape.append(jax.ShapeDtypeStruct((rows, cols), BF16))
    return in_specs, out_specs, out_shape


def _round_slabs(src_refs, dst_refs):
    for src, dst in zip(src_refs, dst_refs):
        dst[...] = src[...].astype(BF16)


@functools.lru_cache(maxsize=None)
def _gla_constants(c):
    i = np.arange(c)[:, None]
    t = np.arange(c)[None, :]
    mats = [t <= i, t > i]
    masks = []
    s = c // 2
    while s >= 1:
        pos = i % (2 * s)
        ref_row = i - pos + s
        upper = pos >= s
        mats.append(np.where(upper, (t > ref_row) & (t <= i), (t > i) & (t <= ref_row)))
        same_block = (i // (2 * s)) == (t // (2 * s))
        masks.append(same_block & upper & ((t % (2 * s)) < s))
        s //= 2
    masks.append(i == t)
    sums = np.concatenate(mats, axis=0).astype(np.float32)
    return sums, np.stack(masks).astype(np.float32)


def _gla_kernel(q_ref, k_ref, v_ref, r_ref, glr_ref, wg_ref, bg_ref, go_ref, sums_ref, mask_ref,
                *refs, n_jobs):
    o_ref, st_ref = refs[n_jobs], refs[-1]
    _round_slabs(refs[:n_jobs], refs[n_jobs + 1:-1])
    c = GLA_CHUNK
    n_levels = mask_ref.shape[0] - 1

    @pl.when(pl.program_id(0) == 0)
    def _():
        st_ref[...] = jnp.zeros_like(st_ref)

    gate_in = jnp.dot(glr_ref[...].astype(BF16), wg_ref[...], preferred_element_type=F32)
    gate_in = gate_in + bg_ref[...]
    log_sig = jnp.minimum(gate_in, 0.0) - jnp.log1p(jnp.exp(-jnp.abs(gate_in)))
    log_a = log_sig * (LOG2_E / GLA_TAU)

    la_hi = log_a.astype(BF16)
    la_lo = (log_a - la_hi.astype(F32)).astype(BF16)
    chunk_sums = sums_ref[0:2 * c, :]
    part_chunk = (jnp.dot(chunk_sums, la_hi, preferred_element_type=F32)
                  + jnp.dot(chunk_sums, la_lo, preferred_element_type=F32))
    part_level = jnp.dot(sums_ref[2 * c:, :], la_hi, preferred_element_type=F32)

    for h in range(GLA_HEADS):
        qk_cols = slice(h * GLA_DK, (h + 1) * GLA_DK)
        v_cols = slice(h * GLA_DV, (h + 1) * GLA_DV)
        q = q_ref[:, qk_cols] * (GLA_DK ** -0.5)
        k = k_ref[:, qk_cols]
        v = v_ref[:, v_cols]

        b = part_chunk[0:c, qk_cols]
        state_t = st_ref[h]
        o = _dot_nt(q * jnp.exp2(b).astype(BF16), state_t.astype(BF16))

        attn = _dot_nt(q, k).astype(BF16) * mask_ref[n_levels]
        for lvl in range(n_levels):
            e = jnp.exp2(part_level[lvl * c:(lvl + 1) * c, qk_cols]).astype(BF16)
            attn = attn + _dot_nt(q * e, k * e).astype(BF16) * mask_ref[lvl]
        o = o + jnp.dot(attn, v, preferred_element_type=F32)

        k_dec = k * jnp.exp2(part_chunk[c:2 * c, qk_cols]).astype(BF16)
        st_ref[h] = state_t * jnp.exp2(b[c - 1:c, :]) + _dot_tn(v, k_dec)

        o = _rmsnorm_rows(o, go_ref[...])
        o_ref[:, v_cols] = (o * r_ref[:, v_cols].astype(F32)).astype(o_ref.dtype)


def _gla_core(proj, glr, w_g2, b_g, g_o, rounding_jobs):
    s = proj.shape[0]
    c = GLA_CHUNK
    w_in_specs, w_out_specs, w_out_shape = _rounding_specs(rounding_jobs, s // c)
    sums, masks = _gla_constants(c)
    sums = jnp.asarray(sums, dtype=BF16)
    masks = jnp.asarray(masks, dtype=BF16)
    w_g2p = jnp.pad(w_g2, ((0, LANES - GLA_RANK), (0, 0))).astype(BF16)
    assert 2 * GLA_QK == GLA_VV
    return pl.pallas_call(
        functools.partial(_gla_kernel, n_jobs=len(rounding_jobs)),
        grid=(s // c,),
        in_specs=[
            pl.BlockSpec((c, GLA_QK), lambda i: (i, 0)),
            pl.BlockSpec((c, GLA_QK), lambda i: (i, 1)),
            pl.BlockSpec((c, GLA_VV), lambda i: (i, 1)),
            pl.BlockSpec((c, GLA_VV), lambda i: (i, 2)),
            pl.BlockSpec((c, LANES), lambda i: (i, 0)),
            pl.BlockSpec((LANES, GLA_QK), lambda i: (0, 0)),
            pl.BlockSpec((1, GLA_QK), lambda i: (0, 0)),
            pl.BlockSpec((1, GLA_DV), lambda i: (0, 0)),
            pl.BlockSpec(sums.shape, lambda i: (0, 0)),
            pl.BlockSpec(masks.shape, lambda i: (0, 0, 0)),
        ] + w_in_specs,
        out_specs=[pl.BlockSpec((c, GLA_VV), lambda i: (i, 0))] + w_out_specs,
        out_shape=[jax.ShapeDtypeStruct((s, GLA_VV), BF16)] + w_out_shape,
        scratch_shapes=[pltpu.VMEM((GLA_HEADS, GLA_DV, GLA_DK), F32)],
        compiler_params=_params(1),
        name="gla_core",
    )(proj, proj, proj, proj, glr, w_g2p, b_g.reshape(1, GLA_QK), g_o.reshape(1, GLA_DV),
      sums, masks, *[job[0] for job in rounding_jobs])


@functools.lru_cache(maxsize=None)
def _swa_bias():
    blk = SWA_BLOCK
    kj = np.arange(2 * blk)[:, None]
    qi = np.arange(blk)[None, :]
    rel = qi + blk - kj
    valid = (rel >= 0) & (rel < SWA_WINDOW)
    first = valid & (kj >= blk)
    planes = np.stack([first, valid])
    bias = np.where(planes, 0.0, -np.inf).astype(np.float32)
    return np.tile(bias, (1, 1, SWA_GROUP))


def _swa_kernel(qt_ref, kp_ref, kc_ref, vtp_ref, vtc_ref, sink_ref, bias_ref, *refs, n_jobs):
    o_ref = refs[n_jobs]
    _round_slabs(refs[:n_jobs], refs[n_jobs + 1:])
    hd = SWA_HD
    blk = SWA_BLOCK
    k_all = jnp.concatenate([kp_ref[...], kc_ref[...]], axis=0)
    vt_all = jnp.concatenate([vtp_ref[...], vtc_ref[...]], axis=1)
    for sub in range(SWA_BLOCKS_PER_STEP):
        q_cols = slice(sub * blk, (sub + 1) * blk)
        k_band = k_all[sub * blk:(sub + 2) * blk, :]
        vt_band = vt_all[:, sub * blk:(sub + 2) * blk]
        if sub == 0:
            bias = bias_ref[jnp.minimum(pl.program_id(0), 1)]
        else:
            bias = bias_ref[1]
        for kvh in range(SWA_KVH):
            cols = [qt_ref[(kvh * SWA_GROUP + g) * hd:(kvh * SWA_GROUP + g + 1) * hd, q_cols]
                    for g in range(SWA_GROUP)]
            q_cat = jnp.concatenate(cols, axis=1)
            s_t = jnp.dot(k_band[:, kvh * hd:(kvh + 1) * hd], q_cat,
                          preferred_element_type=F32) + bias
            sink = sink_ref[kvh:kvh + 1, :] * LOG2_E
            m = jnp.maximum(jnp.max(s_t, axis=0, keepdims=True), sink)
            e = jnp.exp2(s_t - m)
            denom = jnp.sum(e, axis=0, keepdims=True) + jnp.exp2(sink - m)
            o_t = jnp.dot(vt_band[kvh * hd:(kvh + 1) * hd, :], e.astype(BF16),
                          preferred_element_type=F32)
            o_t = o_t * (1.0 / denom)
            for g in range(SWA_GROUP):
                r0 = (kvh * SWA_GROUP + g) * hd
                o_ref[r0:r0 + hd, q_cols] = o_t[:, g * blk:(g + 1) * blk].astype(o_ref.dtype)


def _swa_core(qt, k, vt, sinks, rounding_jobs):
    s = qt.shape[1]
    blk = SWA_BLOCK
    n = SWA_BLOCKS_PER_STEP
    w_in_specs, w_out_specs, w_out_shape = _rounding_specs(rounding_jobs, s // (n * blk))
    bias = jnp.asarray(_swa_bias())
    sink_rows = jnp.repeat(sinks.reshape(SWA_KVH, SWA_GROUP), blk, axis=1)

    def prev(i):
        return jnp.maximum(i * n - 1, 0)

    return pl.pallas_call(
        functools.partial(_swa_kernel, n_jobs=len(rounding_jobs)),
        grid=(s // (n * blk),),
        in_specs=[
            pl.BlockSpec((D_MODEL, n * blk), lambda i: (0, i)),
            pl.BlockSpec((blk, SWA_KVW), lambda i: (prev(i), 0)),
            pl.BlockSpec((n * blk, SWA_KVW), lambda i: (i, 0)),
            pl.BlockSpec((SWA_KVW, blk), lambda i: (0, prev(i))),
            pl.BlockSpec((SWA_KVW, n * blk), lambda i: (0, i)),
            pl.BlockSpec((SWA_KVH, SWA_GROUP * blk), lambda i: (0, 0)),
            pl.BlockSpec(bias.shape, lambda i: (0, 0, 0)),
        ] + w_in_specs,
        out_specs=[pl.BlockSpec((D_MODEL, n * blk), lambda i: (0, i))] + w_out_specs,
        out_shape=[jax.ShapeDtypeStruct((D_MODEL, s), BF16)] + w_out_shape,
        compiler_params=_params(1),
        name="swa_core",
    )(qt, k, k, vt, vt, sink_rows, bias, *[job[0] for job in rounding_jobs])


def kernel(x, norm_mix, norm_mlp, mlp_w1, mlp_w2, a_w_in, a_w_g2, a_b_g, a_g_o, a_w_o, kv_norm,
           kv_w_k, kv_w_v, kv_g_k, b_w_q, b_g_q, b_sinks, b_w_o):
    assert x.shape == (1, SEQ, D_MODEL)
    h = x[0]
    k_shared = vt_shared = None
    wq_next = None
    w_in_next = None
    a_w_in_t = jnp.swapaxes(a_w_in, 1, 2)
    for layer in range(DEPTH):
        jobs = [(mlp_w1, layer), (mlp_w2, layer)]
        if layer < N_GLA:
            jobs.append((a_w_o, layer))
            if layer + 1 == N_GLA:
                jobs.append((b_w_q, 0))
            else:
                jobs.append((a_w_in_t, layer + 1, GLA_MAIN))
            proj, glr = _gla_inproj(h, norm_mix[layer], a_w_in_t, layer, tm=1024,
                                    tn=1024 if w_in_next is None else 2048, w_main=w_in_next)
            o, w1b, w2b, wob, *rest = _gla_core(proj, glr, a_w_g2[layer], a_b_g[layer],
                                                a_g_o[layer], jobs)
            h = _matmul_residual(o, wob, h, tm=512)
            if layer + 1 == N_GLA:
                (wq_next,) = rest
            else:
                (w_in_next,) = rest
        else:
            j = layer - N_GLA
            jobs.append((b_w_o, j))
            if layer + 1 < DEPTH:
                jobs.append((b_w_q, j + 1))
            kv = (kv_norm, kv_w_k, kv_w_v, kv_g_k) if k_shared is None else None
            qt, *shared = _swa_proj(h, norm_mix[layer], wq_next, b_g_q[j], None, tm=1024, tn=1024,
                                    kv=kv)
            if kv is not None:
                k_shared, vt_shared = shared
            ot, w1b, w2b, wob, *rest = _swa_core(qt, k_shared, vt_shared, b_sinks[j], jobs)
            h = _matmul_residual(ot, wob, h, tm=512, a_transposed=True)
            if rest:
                (wq_next,) = rest
        h = _mlp(h, norm_mlp[layer], w1b, w2b, tm=1024, tf=1024)
    return h[None]
```

```python
import functools

import jax
import jax.numpy as jnp
import numpy as np
from jax import lax
from jax.experimental import pallas as pl
from jax.experimental.pallas import tpu as pltpu

F32 = jnp.float32
BF16 = jnp.bfloat16

EPS = 1e-6
LOG2_E = 1.4426950408889634
D_MODEL = 2048
SEQ = 8192
DEPTH = 4
N_GLA = DEPTH // 2
D_FF = 4 * D_MODEL

GLA_HEADS = 4
GLA_DK = 256
GLA_DV = 512
GLA_RANK = 16
GLA_TAU = 16.0
GLA_QK = GLA_HEADS * GLA_DK
GLA_VV = GLA_HEADS * GLA_DV
GLA_MAIN = 2 * GLA_QK + 2 * GLA_VV
GLA_CHUNK = 256
LANES = 128
BF16_ROWS_PER_VREG = 16
ROWS_PER_SLAB = 256

SWA_HD = 64
SWA_QH = 32
SWA_KVH = 4
SWA_GROUP = SWA_QH // SWA_KVH
SWA_WINDOW = 128
SWA_BLOCK = 128
SWA_KVW = SWA_KVH * SWA_HD
SWA_BLOCKS_PER_STEP = 4

V7X_VMEM_BYTES = 64 * 1024 * 1024
VMEM_LIMIT = V7X_VMEM_BYTES - 2 * 1024 * 1024

TM = 1024
TN = 1024
TN_WIDE = 2048
TF = 1024
TM_OUT = 512


def _params(n_axes):
    return pltpu.CompilerParams(
        dimension_semantics=("arbitrary",) * n_axes, vmem_limit_bytes=VMEM_LIMIT)


def _rmsnorm_rows(x, g):
    ms = jnp.mean(x * x, axis=-1, keepdims=True)
    return x * lax.rsqrt(ms + EPS) * g


def _dot_nt(a, b):
    return lax.dot_general(a, b, (((1,), (1,)), ((), ())), preferred_element_type=F32)


def _dot_tn(a, b):
    return lax.dot_general(a, b, (((0,), (0,)), ((), ())), preferred_element_type=F32)


def _row_slabs(n_rows):
    return [slice(r, r + ROWS_PER_SLAB) for r in range(0, n_rows, ROWS_PER_SLAB)]


def _dot_wt(w, hn):
    return lax.dot_general(w, hn, (((0,), (1,)), ((), ())), preferred_element_type=F32)


def _gla_inproj_kernel(x_ref, g_ref, wt_ref, wgt_ref, o_ref, glr_ref, hn_ref, *, first_gate_tile):
    @pl.when(pl.program_id(1) == 0)
    def _():
        w = wt_ref[...].astype(BF16)
        wg = wgt_ref[...]
        row = lax.broadcasted_iota(jnp.int32, wg.shape, 0)
        wg = jnp.where(row < GLA_RANK, wg, 0.0).astype(BF16)
        for sl in _row_slabs(x_ref.shape[0]):
            hn = _rmsnorm_rows(x_ref[sl, :], g_ref[...]).astype(BF16)
            hn_ref[sl, :] = hn
            glr_ref[sl, :] = _dot_nt(hn, wg)
            o_ref[sl, :] = _dot_nt(hn, w).astype(o_ref.dtype)

    is_gate_tile = pl.program_id(1) >= first_gate_tile

    @pl.when((pl.program_id(1) != 0) & jnp.logical_not(is_gate_tile))
    def _():
        y = _dot_nt(hn_ref[...], wt_ref[...].astype(BF16))
        o_ref[...] = y.astype(o_ref.dtype)

    @pl.when(is_gate_tile)
    def _():
        y = _dot_nt(hn_ref[...], wt_ref[...].astype(BF16))
        o_ref[...] = (y * jax.nn.sigmoid(y)).astype(o_ref.dtype)


def _gla_inproj(x, g, wt, layer, tm, tn, w_main=None):
    s, d = x.shape
    main_spec = (pl.BlockSpec((None, tn, d), lambda i, j: (layer, j, 0)) if w_main is None
                 else pl.BlockSpec((tn, d), lambda i, j: (j, 0)))
    first_gate_tile = (2 * GLA_QK + GLA_VV) // tn
    assert first_gate_tile >= 1 and (2 * GLA_QK + GLA_VV) % tn == 0
    return pl.pallas_call(
        functools.partial(_gla_inproj_kernel, first_gate_tile=first_gate_tile),
        grid=(s // tm, GLA_MAIN // tn),
        in_specs=[
            pl.BlockSpec((tm, d), lambda i, j: (i, 0)),
            pl.BlockSpec((1, d), lambda i, j: (0, 0)),
            main_spec,
            pl.BlockSpec((None, LANES, d), lambda i, j: (layer, GLA_MAIN // LANES, 0)),
        ],
        out_specs=[
            pl.BlockSpec((tm, tn), lambda i, j: (i, j)),
            pl.BlockSpec((tm, LANES), lambda i, j: (i, 0)),
        ],
        out_shape=[
            jax.ShapeDtypeStruct((s, GLA_MAIN), BF16),
            jax.ShapeDtypeStruct((s, LANES), F32),
        ],
        scratch_shapes=[pltpu.VMEM((tm, d), BF16)],
        compiler_params=_params(2),
        name="gla_inproj",
    )(x, g.reshape(1, d), wt if w_main is None else w_main, wt)


def _swa_proj_kernel(*refs, with_kv):
    if with_kv:
        (x_ref, gq_ref, wq_ref, ghq_ref, gkv_ref, wk_ref, wv_ref, gk_ref,
         qt_ref, k_ref, vt_ref, hnq_ref) = refs
    else:
        x_ref, gq_ref, wq_ref, ghq_ref, qt_ref, hnq_ref = refs

    def head_normed(q_t):
        heads = []
        for r0 in range(0, q_t.shape[0], SWA_HD):
            q_h = q_t[r0:r0 + SWA_HD, :]
            ms = jnp.mean(q_h * q_h, axis=0, keepdims=True)
            scale = lax.rsqrt(ms + EPS) * (SWA_HD ** -0.5 * LOG2_E)
            heads.append(q_h * scale * ghq_ref[...])
        return jnp.concatenate(heads, axis=0).astype(qt_ref.dtype)

    @pl.when(pl.program_id(1) == 0)
    def _():
        wq = wq_ref[...].astype(BF16)
        if with_kv:
            wk = wk_ref[...].astype(BF16)
            wv = wv_ref[...].astype(BF16)
        for sl in _row_slabs(x_ref.shape[0]):
            x = x_ref[sl, :]
            y = x * lax.rsqrt(jnp.mean(x * x, axis=-1, keepdims=True) + EPS)
            hn_q = (y * gq_ref[...]).astype(BF16)
            hnq_ref[sl, :] = hn_q
            qt_ref[:, sl] = head_normed(_dot_wt(wq, hn_q))
            if with_kv:
                hn_kv = (y * gkv_ref[...]).astype(BF16)
                k = jnp.dot(hn_kv, wk, preferred_element_type=F32)
                heads = [_rmsnorm_rows(k[:, h * SWA_HD:(h + 1) * SWA_HD], gk_ref[...])
                         for h in range(SWA_KVH)]
                k_ref[sl, :] = jnp.concatenate(heads, axis=1).astype(k_ref.dtype)
                vt_ref[:, sl] = _dot_wt(wv, hn_kv).astype(vt_ref.dtype)

    @pl.when(pl.program_id(1) != 0)
    def _():
        qt_ref[...] = head_normed(_dot_wt(wq_ref[...].astype(BF16), hnq_ref[...]))


def _swa_proj(x, g_q, w_q, g_hq, tm, tn, kv=None):
    s, d = x.shape
    n = w_q.shape[1]
    in_specs = [
        pl.BlockSpec((tm, d), lambda i, j: (i, 0)),
        pl.BlockSpec((1, d), lambda i, j: (0, 0)),
        pl.BlockSpec((d, tn), lambda i, j: (0, j)),
        pl.BlockSpec((SWA_HD, 1), lambda i, j: (0, 0)),
    ]
    args = [x, g_q.reshape(1, d), w_q, g_hq.reshape(SWA_HD, 1)]
    out_specs = [pl.BlockSpec((tn, tm), lambda i, j: (j, i))]
    out_shape = [jax.ShapeDtypeStruct((n, s), BF16)]
    scratch = [pltpu.VMEM((tm, d), BF16)]
    if kv is not None:
        g_kv, w_k, w_v, g_k = kv
        in_specs += [
            pl.BlockSpec((1, d), lambda i, j: (0, 0)),
            pl.BlockSpec((d, SWA_KVW), lambda i, j: (0, 0)),
            pl.BlockSpec((d, SWA_KVW), lambda i, j: (0, 0)),
            pl.BlockSpec((1, SWA_HD), lambda i, j: (0, 0)),
        ]
        args += [g_kv.reshape(1, d), w_k, w_v, g_k.reshape(1, SWA_HD)]
        out_specs += [
            pl.BlockSpec((tm, SWA_KVW), lambda i, j: (i, 0)),
            pl.BlockSpec((SWA_KVW, tm), lambda i, j: (0, i)),
        ]
        out_shape += [
            jax.ShapeDtypeStruct((s, SWA_KVW), BF16),
            jax.ShapeDtypeStruct((SWA_KVW, s), BF16),
        ]
    return pl.pallas_call(
        functools.partial(_swa_proj_kernel, with_kv=kv is not None),
        grid=(s // tm, n // tn),
        in_specs=in_specs,
        out_specs=out_specs,
        out_shape=out_shape,
        scratch_shapes=scratch,
        compiler_params=_params(2),
        name="swa_proj",
    )(*args)


def _matmul_residual_kernel(a_ref, w_ref, r_ref, o_ref, *, a_transposed):
    if a_transposed:
        y = _dot_tn(a_ref[...], w_ref[...])
    else:
        y = jnp.dot(a_ref[...], w_ref[...], preferred_element_type=F32)
    o_ref[...] = r_ref[...] + y


def _matmul_residual(a, w, res, tm, a_transposed=False):
    k, n = w.shape
    s = res.shape[0]
    a_spec = (pl.BlockSpec((k, tm), lambda i: (0, i)) if a_transposed
              else pl.BlockSpec((tm, k), lambda i: (i, 0)))
    return pl.pallas_call(
        functools.partial(_matmul_residual_kernel, a_transposed=a_transposed),
        grid=(s // tm,),
        in_specs=[
            a_spec,
            pl.BlockSpec((k, n), lambda i: (0, 0)),
            pl.BlockSpec((tm, n), lambda i: (i, 0)),
        ],
        out_specs=pl.BlockSpec((tm, n), lambda i: (i, 0)),
        out_shape=jax.ShapeDtypeStruct((s, n), F32),
        compiler_params=_params(1),
        name="matmul_residual",
    )(a, w, res)


def _mlp_kernel(x_ref, g_ref, w1_ref, w2_ref, o_ref, hn_ref):
    def hidden_out(hn, w1, w2):
        u = jnp.maximum(jnp.dot(hn, w1, preferred_element_type=F32), 0.0)
        return jnp.dot((u * u).astype(BF16), w2, preferred_element_type=F32)

    @pl.when(pl.program_id(1) == 0)
    def _():
        w1 = w1_ref[...].astype(BF16)
        w2 = w2_ref[...].astype(BF16)
        for sl in _row_slabs(x_ref.shape[0]):
            x = x_ref[sl, :]
            hn = _rmsnorm_rows(x, g_ref[...]).astype(BF16)
            hn_ref[sl, :] = hn
            o_ref[sl, :] = x + hidden_out(hn, w1, w2)

    @pl.when(pl.program_id(1) != 0)
    def _():
        o_ref[...] += hidden_out(hn_ref[...], w1_ref[...].astype(BF16), w2_ref[...].astype(BF16))


def _mlp(x, g, w1, w2, tm, tf):
    s, d = x.shape
    f = w1.shape[1]
    return pl.pallas_call(
        _mlp_kernel,
        grid=(s // tm, f // tf),
        in_specs=[
            pl.BlockSpec((tm, d), lambda i, j: (i, 0)),
            pl.BlockSpec((1, d), lambda i, j: (0, 0)),
            pl.BlockSpec((d, tf), lambda i, j: (0, j)),
            pl.BlockSpec((tf, d), lambda i, j: (j, 0)),
        ],
        out_specs=pl.BlockSpec((tm, d), lambda i, j: (i, 0)),
        out_shape=jax.ShapeDtypeStruct((s, d), F32),
        scratch_shapes=[pltpu.VMEM((tm, d), BF16)],
        compiler_params=_params(2),
        name="mlp",
    )(x, g.reshape(1, d), w1, w2)


def _rounding_specs(jobs, n_steps):
    in_specs, out_specs, out_shape = [], [], []
    for w, layer, *limit in jobs:
        _, rows, cols = w.shape
        if limit:
            (rows,) = limit
        slab = rows // n_steps
        assert rows % n_steps == 0 and slab % BF16_ROWS_PER_VREG == 0
        in_specs.append(pl.BlockSpec((None, slab, cols), lambda i, layer=layer: (layer, i, 0)))
        out_specs.append(pl.BlockSpec((slab, cols), lambda i: (i, 0)))
        out_shape.append(jax.ShapeDtypeStruct((rows, cols), BF16))
    return in_specs, out_specs, out_shape


def _round_slabs(src_refs, dst_refs):
    for src, dst in zip(src_refs, dst_refs):
        dst[...] = src[...].astype(BF16)


@functools.lru_cache(maxsize=None)
def _gla_constants(c):
    i = np.arange(c)[:, None]
    t = np.arange(c)[None, :]
    mats = [t <= i, t > i]
    masks = []
    s = c // 2
    while s >= 1:
        pos = i % (2 * s)
        ref_row = i - pos + s
        upper = pos >= s
        mats.append(np.where(upper, (t > ref_row) & (t <= i), (t > i) & (t <= ref_row)))
        same_block = (i // (2 * s)) == (t // (2 * s))
        masks.append(same_block & upper & ((t % (2 * s)) < s))
        s //= 2
    masks.append(i == t)
    sums = np.concatenate(mats, axis=0).astype(np.float32)
    return sums, np.stack(masks).astype(np.float32)


def _gla_kernel(q_ref, k_ref, v_ref, r_ref, glr_ref, wg_ref, bg_ref, go_ref, sums_ref, mask_ref,
                *refs, n_jobs):
    o_ref, st_ref = refs[n_jobs], refs[-1]
    _round_slabs(refs[:n_jobs], refs[n_jobs + 1:-1])
    c = GLA_CHUNK
    n_levels = mask_ref.shape[0] - 1

    @pl.when(pl.program_id(0) == 0)
    def _():
        st_ref[...] = jnp.zeros_like(st_ref)

    gate_in = jnp.dot(glr_ref[...].astype(BF16), wg_ref[...], preferred_element_type=F32)
    gate_in = gate_in + bg_ref[...]
    log_sig = jnp.minimum(gate_in, 0.0) - jnp.log1p(jnp.exp(-jnp.abs(gate_in)))
    log_a = log_sig * (LOG2_E / GLA_TAU)

    la_hi = log_a.astype(BF16)
    la_lo = (log_a - la_hi.astype(F32)).astype(BF16)
    chunk_sums = sums_ref[0:2 * c, :]
    part_chunk = (jnp.dot(chunk_sums, la_hi, preferred_element_type=F32)
                  + jnp.dot(chunk_sums, la_lo, preferred_element_type=F32))
    part_level = jnp.dot(sums_ref[2 * c:, :], la_hi, preferred_element_type=F32)

    for h in range(GLA_HEADS):
        qk_cols = slice(h * GLA_DK, (h + 1) * GLA_DK)
        v_cols = slice(h * GLA_DV, (h + 1) * GLA_DV)
        q = q_ref[:, qk_cols] * (GLA_DK ** -0.5)
        k = k_ref[:, qk_cols]
        v = v_ref[:, v_cols]

        b = part_chunk[0:c, qk_cols]
        state_t = st_ref[h]
        o = _dot_nt(q * jnp.exp2(b).astype(BF16), state_t.astype(BF16))

        attn = _dot_nt(q, k).astype(BF16) * mask_ref[n_levels]
        for lvl in range(n_levels):
            e = jnp.exp2(part_level[lvl * c:(lvl + 1) * c, qk_cols]).astype(BF16)
            attn = attn + _dot_nt(q * e, k * e).astype(BF16) * mask_ref[lvl]
        o = o + jnp.dot(attn, v, preferred_element_type=F32)

        k_dec = k * jnp.exp2(part_chunk[c:2 * c, qk_cols]).astype(BF16)
        st_ref[h] = state_t * jnp.exp2(b[c - 1:c, :]) + _dot_tn(v, k_dec)

        o = _rmsnorm_rows(o, go_ref[...])
        o_ref[:, v_cols] = (o * r_ref[:, v_cols].astype(F32)).astype(o_ref.dtype)


def _gla_core(proj, glr, w_g2, b_g, g_o, rounding_jobs):
    s = proj.shape[0]
    c = GLA_CHUNK
    w_in_specs, w_out_specs, w_out_shape = _rounding_specs(rounding_jobs, s // c)
    sums, masks = _gla_constants(c)
    sums = jnp.asarray(sums, dtype=BF16)
    masks = jnp.asarray(masks, dtype=BF16)
    w_g2p = jnp.pad(w_g2, ((0, LANES - GLA_RANK), (0, 0))).astype(BF16)
    assert 2 * GLA_QK == GLA_VV
    return pl.pallas_call(
        functools.partial(_gla_kernel, n_jobs=len(rounding_jobs)),
        grid=(s // c,),
        in_specs=[
            pl.BlockSpec((c, GLA_QK), lambda i: (i, 0)),
            pl.BlockSpec((c, GLA_QK), lambda i: (i, 1)),
            pl.BlockSpec((c, GLA_VV), lambda i: (i, 1)),
            pl.BlockSpec((c, GLA_VV), lambda i: (i, 2)),
            pl.BlockSpec((c, LANES), lambda i: (i, 0)),
            pl.BlockSpec((LANES, GLA_QK), lambda i: (0, 0)),
            pl.BlockSpec((1, GLA_QK), lambda i: (0, 0)),
            pl.BlockSpec((1, GLA_DV), lambda i: (0, 0)),
            pl.BlockSpec(sums.shape, lambda i: (0, 0)),
            pl.BlockSpec(masks.shape, lambda i: (0, 0, 0)),
        ] + w_in_specs,
        out_specs=[pl.BlockSpec((c, GLA_VV), lambda i: (i, 0))] + w_out_specs,
        out_shape=[jax.ShapeDtypeStruct((s, GLA_VV), BF16)] + w_out_shape,
        scratch_shapes=[pltpu.VMEM((GLA_HEADS, GLA_DV, GLA_DK), F32)],
        compiler_params=_params(1),
        name="gla_core",
    )(proj, proj, proj, proj, glr, w_g2p, b_g.reshape(1, GLA_QK), g_o.reshape(1, GLA_DV),
      sums, masks, *[job[0] for job in rounding_jobs])


@functools.lru_cache(maxsize=None)
def _swa_bias():
    blk = SWA_BLOCK
    kj = np.arange(2 * blk)[:, None]
    qi = np.arange(blk)[None, :]
    rel = qi + blk - kj
    valid = (rel >= 0) & (rel < SWA_WINDOW)
    first = valid & (kj >= blk)
    planes = np.stack([first, valid])
    bias = np.where(planes, 0.0, -np.inf).astype(np.float32)
    return np.tile(bias, (1, 1, SWA_GROUP))


def _swa_kernel(qt_ref, kp_ref, kc_ref, vtp_ref, vtc_ref, sink_ref, bias_ref, *refs, n_jobs):
    o_ref = refs[n_jobs]
    _round_slabs(refs[:n_jobs], refs[n_jobs + 1:])
    hd = SWA_HD
    blk = SWA_BLOCK
    k_all = jnp.concatenate([kp_ref[...], kc_ref[...]], axis=0)
    vt_all = jnp.concatenate([vtp_ref[...], vtc_ref[...]], axis=1)
    for sub in range(SWA_BLOCKS_PER_STEP):
        q_cols = slice(sub * blk, (sub + 1) * blk)
        k_band = k_all[sub * blk:(sub + 2) * blk, :]
        vt_band = vt_all[:, sub * blk:(sub + 2) * blk]
        if sub == 0:
            bias = bias_ref[jnp.minimum(pl.program_id(0), 1)]
        else:
            bias = bias_ref[1]
        for kvh in range(SWA_KVH):
            cols = [qt_ref[(kvh * SWA_GROUP + g) * hd:(kvh * SWA_GROUP + g + 1) * hd, q_cols]
                    for g in range(SWA_GROUP)]
            q_cat = jnp.concatenate(cols, axis=1)
            s_t = jnp.dot(k_band[:, kvh * hd:(kvh + 1) * hd], q_cat,
                          preferred_element_type=F32) + bias
            sink = sink_ref[kvh:kvh + 1, :] * LOG2_E
            m = jnp.maximum(jnp.max(s_t, axis=0, keepdims=True), sink)
            e = jnp.exp2(s_t - m)
            denom = jnp.sum(e, axis=0, keepdims=True) + jnp.exp2(sink - m)
            o_t = jnp.dot(vt_band[kvh * hd:(kvh + 1) * hd, :], e.astype(BF16),
                          preferred_element_type=F32)
            o_t = o_t * (1.0 / denom)
            for g in range(SWA_GROUP):
                r0 = (kvh * SWA_GROUP + g) * hd
                o_ref[r0:r0 + hd, q_cols] = o_t[:, g * blk:(g + 1) * blk].astype(o_ref.dtype)


def _swa_core(qt, k, vt, sinks, rounding_jobs):
    s = qt.shape[1]
    blk = SWA_BLOCK
    n = SWA_BLOCKS_PER_STEP
    w_in_specs, w_out_specs, w_out_shape = _rounding_specs(rounding_jobs, s // (n * blk))
    bias = jnp.asarray(_swa_bias())
    sink_rows = jnp.repeat(sinks.reshape(SWA_KVH, SWA_GROUP), blk, axis=1)

    def prev(i):
        return jnp.maximum(i * n - 1, 0)

    return pl.pallas_call(
        functools.partial(_swa_kernel, n_jobs=len(rounding_jobs)),
        grid=(s // (n * blk),),
        in_specs=[
            pl.BlockSpec((D_MODEL, n * blk), lambda i: (0, i)),
            pl.BlockSpec((blk, SWA_KVW), lambda i: (prev(i), 0)),
            pl.BlockSpec((n * blk, SWA_KVW), lambda i: (i, 0)),
            pl.BlockSpec((SWA_KVW, blk), lambda i: (0, prev(i))),
            pl.BlockSpec((SWA_KVW, n * blk), lambda i: (0, i)),
            pl.BlockSpec((SWA_KVH, SWA_GROUP * blk), lambda i: (0, 0)),
            pl.BlockSpec(bias.shape, lambda i: (0, 0, 0)),
        ] + w_in_specs,
        out_specs=[pl.BlockSpec((D_MODEL, n * blk), lambda i: (0, i))] + w_out_specs,
        out_shape=[jax.ShapeDtypeStruct((D_MODEL, s), BF16)] + w_out_shape,
        compiler_params=_params(1),
        name="swa_core",
    )(qt, k, k, vt, vt, sink_rows, bias, *[job[0] for job in rounding_jobs])


def kernel(x, norm_mix, norm_mlp, mlp_w1, mlp_w2, a_w_in, a_w_g2, a_b_g, a_g_o, a_w_o, kv_norm,
           kv_w_k, kv_w_v, kv_g_k, b_w_q, b_g_q, b_sinks, b_w_o):
    assert x.shape == (1, SEQ, D_MODEL)
    h = x[0]
    k_shared = vt_shared = None
    wq_next = None
    w_in_next = None
    a_w_in_t = jnp.swapaxes(a_w_in, 1, 2)
    for layer in range(DEPTH):
        jobs = [(mlp_w1, layer), (mlp_w2, layer)]
        if layer < N_GLA:
            jobs.append((a_w_o, layer))
            if layer + 1 == N_GLA:
                jobs.append((b_w_q, 0))
            else:
                jobs.append((a_w_in_t, layer + 1, GLA_MAIN))
            proj, glr = _gla_inproj(h, norm_mix[layer], a_w_in_t, layer, tm=TM,
                                    tn=TN if w_in_next is None else TN_WIDE, w_main=w_in_next)
            o, w1b, w2b, wob, *rest = _gla_core(proj, glr, a_w_g2[layer], a_b_g[layer],
                                                a_g_o[layer], jobs)
            h = _matmul_residual(o, wob, h, tm=TM_OUT)
            if layer + 1 == N_GLA:
                (wq_next,) = rest
            else:
                (w_in_next,) = rest
        else:
            j = layer - N_GLA
            jobs.append((b_w_o, j))
            if layer + 1 < DEPTH:
                jobs.append((b_w_q, j + 1))
            kv = (kv_norm, kv_w_k, kv_w_v, kv_g_k) if k_shared is None else None
            qt, *shared = _swa_proj(h, norm_mix[layer], wq_next, b_g_q[j], tm=TM, tn=TN, kv=kv)
            if kv is not None:
                k_shared, vt_shared = shared
            ot, w1b, w2b, wob, *rest = _swa_core(qt, k_shared, vt_shared, b_sinks[j], jobs)
            h = _matmul_residual(ot, wob, h, tm=TM_OUT, a_transposed=True)
            if rest:
                (wq_next,) = rest
        h = _mlp(h, norm_mlp[layer], w1b, w2b, tm=TM, tf=TF)
    return h[None]
```

```python
import functools

import jax
import jax.numpy as jnp
import numpy as np
from jax import lax
from jax.experimental import pallas as pl
from jax.experimental.pallas import tpu as pltpu

F32 = jnp.float32
BF16 = jnp.bfloat16

EPS = 1e-6
LOG2_E = 1.4426950408889634
D_MODEL = 2048
SEQ = 8192
DEPTH = 4
N_GLA = DEPTH // 2
D_FF = 4 * D_MODEL

GLA_HEADS = 4
GLA_DK = 256
GLA_DV = 512
GLA_RANK = 16
GLA_TAU = 16.0
GLA_QK = GLA_HEADS * GLA_DK
GLA_VV = GLA_HEADS * GLA_DV
GLA_MAIN = 2 * GLA_QK + 2 * GLA_VV
GLA_CHUNK = 256
LANES = 128
BF16_ROWS_PER_VREG = 16
ROWS_PER_SLAB = 256

SWA_HD = 64
SWA_QH = 32
SWA_KVH = 4
SWA_GROUP = SWA_QH // SWA_KVH
SWA_WINDOW = 128
SWA_BLOCK = 128
SWA_KVW = SWA_KVH * SWA_HD
SWA_BLOCKS_PER_STEP = 4

V7X_VMEM_BYTES = 64 * 1024 * 1024
VMEM_LIMIT = V7X_VMEM_BYTES - 2 * 1024 * 1024

TM = 1024
TN = 1024
TN_WIDE = 2048
TF = 1024
TM_OUT = 512


def _params(n_axes):
    return pltpu.CompilerParams(
        dimension_semantics=("arbitrary",) * n_axes, vmem_limit_bytes=VMEM_LIMIT)


def _rmsnorm_rows(x, g):
    ms = jnp.mean(x * x, axis=-1, keepdims=True)
    return x * lax.rsqrt(ms + EPS) * g


def _dot_nt(a, b):
    return lax.dot_general(a, b, (((1,), (1,)), ((), ())), preferred_element_type=F32)


def _dot_tn(a, b):
    return lax.dot_general(a, b, (((0,), (0,)), ((), ())), preferred_element_type=F32)


def _row_slabs(n_rows):
    return [slice(r, r + ROWS_PER_SLAB) for r in range(0, n_rows, ROWS_PER_SLAB)]


def _dot_wt(w, hn):
    return lax.dot_general(w, hn, (((0,), (1,)), ((), ())), preferred_element_type=F32)


def _gla_inproj_kernel(x_ref, g_ref, wt_ref, wgt_ref, wo_ref, o_ref, glr_ref, wob_ref, hn_ref, *,
                       first_gate_tile):
    @pl.when(pl.program_id(1) == 0)
    def _():
        wob_ref[...] = wo_ref[...].astype(BF16)
        w = wt_ref[...].astype(BF16)
        wg = wgt_ref[...]
        row = lax.broadcasted_iota(jnp.int32, wg.shape, 0)
        wg = jnp.where(row < GLA_RANK, wg, 0.0).astype(BF16)
        for sl in _row_slabs(x_ref.shape[0]):
            hn = _rmsnorm_rows(x_ref[sl, :], g_ref[...]).astype(BF16)
            hn_ref[sl, :] = hn
            glr_ref[sl, :] = _dot_nt(hn, wg)
            o_ref[sl, :] = _dot_nt(hn, w).astype(o_ref.dtype)

    is_gate_tile = pl.program_id(1) >= first_gate_tile

    @pl.when((pl.program_id(1) != 0) & jnp.logical_not(is_gate_tile))
    def _():
        y = _dot_nt(hn_ref[...], wt_ref[...].astype(BF16))
        o_ref[...] = y.astype(o_ref.dtype)

    @pl.when(is_gate_tile)
    def _():
        y = _dot_nt(hn_ref[...], wt_ref[...].astype(BF16))
        o_ref[...] = (y * jax.nn.sigmoid(y)).astype(o_ref.dtype)


def _gla_inproj(x, g, wt, w_o, layer, tm, tn, w_main=None):
    s, d = x.shape
    wo_rows = w_o.shape[1] // (s // tm)
    assert w_o.shape[1] % (s // tm) == 0 and wo_rows % BF16_ROWS_PER_VREG == 0
    main_spec = (pl.BlockSpec((None, tn, d), lambda i, j: (layer, j, 0)) if w_main is None
                 else pl.BlockSpec((tn, d), lambda i, j: (j, 0)))
    first_gate_tile = (2 * GLA_QK + GLA_VV) // tn
    assert first_gate_tile >= 1 and (2 * GLA_QK + GLA_VV) % tn == 0
    return pl.pallas_call(
        functools.partial(_gla_inproj_kernel, first_gate_tile=first_gate_tile),
        grid=(s // tm, GLA_MAIN // tn),
        in_specs=[
            pl.BlockSpec((tm, d), lambda i, j: (i, 0)),
            pl.BlockSpec((1, d), lambda i, j: (0, 0)),
            main_spec,
            pl.BlockSpec((None, LANES, d), lambda i, j: (layer, GLA_MAIN // LANES, 0)),
            pl.BlockSpec((None, wo_rows, w_o.shape[2]), lambda i, j: (layer, i, 0)),
        ],
        out_specs=[
            pl.BlockSpec((tm, tn), lambda i, j: (i, j)),
            pl.BlockSpec((tm, LANES), lambda i, j: (i, 0)),
            pl.BlockSpec((wo_rows, w_o.shape[2]), lambda i, j: (i, 0)),
        ],
        out_shape=[
            jax.ShapeDtypeStruct((s, GLA_MAIN), BF16),
            jax.ShapeDtypeStruct((s, LANES), F32),
            jax.ShapeDtypeStruct(w_o.shape[1:], BF16),
        ],
        scratch_shapes=[pltpu.VMEM((tm, d), BF16)],
        compiler_params=_params(2),
        name="gla_inproj",
    )(x, g.reshape(1, d), wt if w_main is None else w_main, wt, w_o)


def _swa_proj_kernel(*refs, with_kv):
    if with_kv:
        (x_ref, gq_ref, wq_ref, ghq_ref, gkv_ref, wk_ref, wv_ref, gk_ref,
         qt_ref, k_ref, vt_ref, hnq_ref) = refs
    else:
        x_ref, gq_ref, wq_ref, ghq_ref, qt_ref, hnq_ref = refs

    def head_normed(q_t):
        heads = []
        for r0 in range(0, q_t.shape[0], SWA_HD):
            q_h = q_t[r0:r0 + SWA_HD, :]
            ms = jnp.mean(q_h * q_h, axis=0, keepdims=True)
            scale = lax.rsqrt(ms + EPS) * (SWA_HD ** -0.5 * LOG2_E)
            heads.append(q_h * scale * ghq_ref[...])
        return jnp.concatenate(heads, axis=0).astype(qt_ref.dtype)

    @pl.when(pl.program_id(1) == 0)
    def _():
        wq = wq_ref[...].astype(BF16)
        if with_kv:
            wk = wk_ref[...].astype(BF16)
            wv = wv_ref[...].astype(BF16)
        for sl in _row_slabs(x_ref.shape[0]):
            x = x_ref[sl, :]
            y = x * lax.rsqrt(jnp.mean(x * x, axis=-1, keepdims=True) + EPS)
            hn_q = (y * gq_ref[...]).astype(BF16)
            hnq_ref[sl, :] = hn_q
            qt_ref[:, sl] = head_normed(_dot_wt(wq, hn_q))
            if with_kv:
                hn_kv = (y * gkv_ref[...]).astype(BF16)
                k = jnp.dot(hn_kv, wk, preferred_element_type=F32)
                heads = [_rmsnorm_rows(k[:, h * SWA_HD:(h + 1) * SWA_HD], gk_ref[...])
                         for h in range(SWA_KVH)]
                k_ref[sl, :] = jnp.concatenate(heads, axis=1).astype(k_ref.dtype)
                vt_ref[:, sl] = _dot_wt(wv, hn_kv).astype(vt_ref.dtype)

    @pl.when(pl.program_id(1) != 0)
    def _():
        qt_ref[...] = head_normed(_dot_wt(wq_ref[...].astype(BF16), hnq_ref[...]))


def _swa_proj(x, g_q, w_q, g_hq, tm, tn, kv=None):
    s, d = x.shape
    n = w_q.shape[1]
    in_specs = [
        pl.BlockSpec((tm, d), lambda i, j: (i, 0)),
        pl.BlockSpec((1, d), lambda i, j: (0, 0)),
        pl.BlockSpec((d, tn), lambda i, j: (0, j)),
        pl.BlockSpec((SWA_HD, 1), lambda i, j: (0, 0)),
    ]
    args = [x, g_q.reshape(1, d), w_q, g_hq.reshape(SWA_HD, 1)]
    out_specs = [pl.BlockSpec((tn, tm), lambda i, j: (j, i))]
    out_shape = [jax.ShapeDtypeStruct((n, s), BF16)]
    scratch = [pltpu.VMEM((tm, d), BF16)]
    if kv is not None:
        g_kv, w_k, w_v, g_k = kv
        in_specs += [
            pl.BlockSpec((1, d), lambda i, j: (0, 0)),
            pl.BlockSpec((d, SWA_KVW), lambda i, j: (0, 0)),
            pl.BlockSpec((d, SWA_KVW), lambda i, j: (0, 0)),
            pl.BlockSpec((1, SWA_HD), lambda i, j: (0, 0)),
        ]
        args += [g_kv.reshape(1, d), w_k, w_v, g_k.reshape(1, SWA_HD)]
        out_specs += [
            pl.BlockSpec((tm, SWA_KVW), lambda i, j: (i, 0)),
            pl.BlockSpec((SWA_KVW, tm), lambda i, j: (0, i)),
        ]
        out_shape += [
            jax.ShapeDtypeStruct((s, SWA_KVW), BF16),
            jax.ShapeDtypeStruct((SWA_KVW, s), BF16),
        ]
    return pl.pallas_call(
        functools.partial(_swa_proj_kernel, with_kv=kv is not None),
        grid=(s // tm, n // tn),
        in_specs=in_specs,
        out_specs=out_specs,
        out_shape=out_shape,
        scratch_shapes=scratch,
        compiler_params=_params(2),
        name="swa_proj",
    )(*args)


def _matmul_residual_kernel(a_ref, w_ref, r_ref, o_ref, *, a_transposed):
    if a_transposed:
        y = _dot_tn(a_ref[...], w_ref[...])
    else:
        y = jnp.dot(a_ref[...], w_ref[...], preferred_element_type=F32)
    o_ref[...] = r_ref[...] + y


def _matmul_residual(a, w, res, tm, a_transposed=False):
    k, n = w.shape
    s = res.shape[0]
    a_spec = (pl.BlockSpec((k, tm), lambda i: (0, i)) if a_transposed
              else pl.BlockSpec((tm, k), lambda i: (i, 0)))
    return pl.pallas_call(
        functools.partial(_matmul_residual_kernel, a_transposed=a_transposed),
        grid=(s // tm,),
        in_specs=[
            a_spec,
            pl.BlockSpec((k, n), lambda i: (0, 0)),
            pl.BlockSpec((tm, n), lambda i: (i, 0)),
        ],
        out_specs=pl.BlockSpec((tm, n), lambda i: (i, 0)),
        out_shape=jax.ShapeDtypeStruct((s, n), F32),
        compiler_params=_params(1),
        name="matmul_residual",
    )(a, w, res)


def _mlp_kernel(x_ref, g_ref, w1_ref, w2_ref, o_ref, hn_ref):
    def hidden_out(hn, w1, w2):
        u = jnp.maximum(jnp.dot(hn, w1, preferred_element_type=F32), 0.0)
        return jnp.dot((u * u).astype(BF16), w2, preferred_element_type=F32)

    @pl.when(pl.program_id(1) == 0)
    def _():
        w1 = w1_ref[...].astype(BF16)
        w2 = w2_ref[...].astype(BF16)
        for sl in _row_slabs(x_ref.shape[0]):
            x = x_ref[sl, :]
            hn = _rmsnorm_rows(x, g_ref[...]).astype(BF16)
            hn_ref[sl, :] = hn
            o_ref[sl, :] = x + hidden_out(hn, w1, w2)

    @pl.when(pl.program_id(1) != 0)
    def _():
        o_ref[...] += hidden_out(hn_ref[...], w1_ref[...].astype(BF16), w2_ref[...].astype(BF16))


def _mlp(x, g, w1, w2, tm, tf):
    s, d = x.shape
    f = w1.shape[1]
    return pl.pallas_call(
        _mlp_kernel,
        grid=(s // tm, f // tf),
        in_specs=[
            pl.BlockSpec((tm, d), lambda i, j: (i, 0)),
            pl.BlockSpec((1, d), lambda i, j: (0, 0)),
            pl.BlockSpec((d, tf), lambda i, j: (0, j)),
            pl.BlockSpec((tf, d), lambda i, j: (j, 0)),
        ],
        out_specs=pl.BlockSpec((tm, d), lambda i, j: (i, 0)),
        out_shape=jax.ShapeDtypeStruct((s, d), F32),
        scratch_shapes=[pltpu.VMEM((tm, d), BF16)],
        compiler_params=_params(2),
        name="mlp",
    )(x, g.reshape(1, d), w1, w2)


def _rounding_specs(jobs, n_steps):
    in_specs, out_specs, out_shape = [], [], []
    for w, layer, *limit in jobs:
        _, rows, cols = w.shape
        if limit:
            (rows,) = limit
        slab = rows // n_steps
        assert rows % n_steps == 0 and slab % BF16_ROWS_PER_VREG == 0
        in_specs.append(pl.BlockSpec((None, slab, cols), lambda i, layer=layer: (layer, i, 0)))
        out_specs.append(pl.BlockSpec((slab, cols), lambda i: (i, 0)))
        out_shape.append(jax.ShapeDtypeStruct((rows, cols), BF16))
    return in_specs, out_specs, out_shape


def _round_slabs(src_refs, dst_refs):
    for src, dst in zip(src_refs, dst_refs):
        dst[...] = src[...].astype(BF16)


@functools.lru_cache(maxsize=None)
def _gla_constants(c):
    i = np.arange(c)[:, None]
    t = np.arange(c)[None, :]
    mats = [t <= i, t > i]
    masks = []
    s = c // 2
    while s >= 1:
        pos = i % (2 * s)
        ref_row = i - pos + s
        upper = pos >= s
        mats.append(np.where(upper, (t > ref_row) & (t <= i), (t > i) & (t <= ref_row)))
        same_block = (i // (2 * s)) == (t // (2 * s))
        masks.append(same_block & upper & ((t % (2 * s)) < s))
        s //= 2
    masks.append(i == t)
    sums = np.concatenate(mats, axis=0).astype(np.float32)
    return sums, np.stack(masks).astype(np.float32)


def _gla_kernel(q_ref, k_ref, v_ref, r_ref, glr_ref, wg_ref, bg_ref, go_ref, sums_ref, mask_ref,
                wo_ref, res_ref, *refs, n_jobs):
    h_ref, st_ref = refs[n_jobs], refs[-1]
    _round_slabs(refs[:n_jobs], refs[n_jobs + 1:-1])
    gated = []
    c = GLA_CHUNK
    n_levels = mask_ref.shape[0] - 1

    @pl.when(pl.program_id(0) == 0)
    def _():
        st_ref[...] = jnp.zeros_like(st_ref)

    gate_in = jnp.dot(glr_ref[...].astype(BF16), wg_ref[...], preferred_element_type=F32)
    gate_in = gate_in + bg_ref[...]
    log_sig = jnp.minimum(gate_in, 0.0) - jnp.log1p(jnp.exp(-jnp.abs(gate_in)))
    log_a = log_sig * (LOG2_E / GLA_TAU)

    la_hi = log_a.astype(BF16)
    la_lo = (log_a - la_hi.astype(F32)).astype(BF16)
    chunk_sums = sums_ref[0:2 * c, :]
    part_chunk = (jnp.dot(chunk_sums, la_hi, preferred_element_type=F32)
                  + jnp.dot(chunk_sums, la_lo, preferred_element_type=F32))
    part_level = jnp.dot(sums_ref[2 * c:, :], la_hi, preferred_element_type=F32)

    for h in range(GLA_HEADS):
        qk_cols = slice(h * GLA_DK, (h + 1) * GLA_DK)
        v_cols = slice(h * GLA_DV, (h + 1) * GLA_DV)
        q = q_ref[:, qk_cols] * (GLA_DK ** -0.5)
        k = k_ref[:, qk_cols]
        v = v_ref[:, v_cols]

        b = part_chunk[0:c, qk_cols]
        state_t = st_ref[h]
        o = _dot_nt(q * jnp.exp2(b).astype(BF16), state_t.astype(BF16))

        attn = _dot_nt(q, k).astype(BF16) * mask_ref[n_levels]
        for lvl in range(n_levels):
            e = jnp.exp2(part_level[lvl * c:(lvl + 1) * c, qk_cols]).astype(BF16)
            attn = attn + _dot_nt(q * e, k * e).astype(BF16) * mask_ref[lvl]
        o = o + jnp.dot(attn, v, preferred_element_type=F32)

        k_dec = k * jnp.exp2(part_chunk[c:2 * c, qk_cols]).astype(BF16)
        st_ref[h] = state_t * jnp.exp2(b[c - 1:c, :]) + _dot_tn(v, k_dec)

        o = _rmsnorm_rows(o, go_ref[...])
        gated.append((o * r_ref[:, v_cols].astype(F32)).astype(BF16))

    h_ref[...] = res_ref[...] + jnp.dot(jnp.concatenate(gated, axis=1), wo_ref[...],
                                        preferred_element_type=F32)


def _gla_core(proj, glr, w_g2, b_g, g_o, w_o, res, rounding_jobs):
    s = proj.shape[0]
    c = GLA_CHUNK
    w_in_specs, w_out_specs, w_out_shape = _rounding_specs(rounding_jobs, s // c)
    sums, masks = _gla_constants(c)
    sums = jnp.asarray(sums, dtype=BF16)
    masks = jnp.asarray(masks, dtype=BF16)
    w_g2p = jnp.pad(w_g2, ((0, LANES - GLA_RANK), (0, 0))).astype(BF16)
    assert 2 * GLA_QK == GLA_VV
    return pl.pallas_call(
        functools.partial(_gla_kernel, n_jobs=len(rounding_jobs)),
        grid=(s // c,),
        in_specs=[
            pl.BlockSpec((c, GLA_QK), lambda i: (i, 0)),
            pl.BlockSpec((c, GLA_QK), lambda i: (i, 1)),
            pl.BlockSpec((c, GLA_VV), lambda i: (i, 1)),
            pl.BlockSpec((c, GLA_VV), lambda i: (i, 2)),
            pl.BlockSpec((c, LANES), lambda i: (i, 0)),
            pl.BlockSpec((LANES, GLA_QK), lambda i: (0, 0)),
            pl.BlockSpec((1, GLA_QK), lambda i: (0, 0)),
            pl.BlockSpec((1, GLA_DV), lambda i: (0, 0)),
            pl.BlockSpec(sums.shape, lambda i: (0, 0)),
            pl.BlockSpec(masks.shape, lambda i: (0, 0, 0)),
            pl.BlockSpec(w_o.shape, lambda i: (0, 0)),
            pl.BlockSpec((c, D_MODEL), lambda i: (i, 0)),
        ] + w_in_specs,
        out_specs=[pl.BlockSpec((c, D_MODEL), lambda i: (i, 0))] + w_out_specs,
        out_shape=[jax.ShapeDtypeStruct((s, D_MODEL), F32)] + w_out_shape,
        scratch_shapes=[pltpu.VMEM((GLA_HEADS, GLA_DV, GLA_DK), F32)],
        compiler_params=_params(1),
        name="gla_core",
    )(proj, proj, proj, proj, glr, w_g2p, b_g.reshape(1, GLA_QK), g_o.reshape(1, GLA_DV),
      sums, masks, w_o, res, *[job[0] for job in rounding_jobs])


@functools.lru_cache(maxsize=None)
def _swa_bias():
    blk = SWA_BLOCK
    kj = np.arange(2 * blk)[:, None]
    qi = np.arange(blk)[None, :]
    rel = qi + blk - kj
    valid = (rel >= 0) & (rel < SWA_WINDOW)
    first = valid & (kj >= blk)
    planes = np.stack([first, valid])
    bias = np.where(planes, 0.0, -np.inf).astype(np.float32)
    return np.tile(bias, (1, 1, SWA_GROUP))


def _swa_kernel(qt_ref, kp_ref, kc_ref, vtp_ref, vtc_ref, sink_ref, bias_ref, *refs, n_jobs):
    o_ref = refs[n_jobs]
    _round_slabs(refs[:n_jobs], refs[n_jobs + 1:])
    hd = SWA_HD
    blk = SWA_BLOCK
    k_all = jnp.concatenate([kp_ref[...], kc_ref[...]], axis=0)
    vt_all = jnp.concatenate([vtp_ref[...], vtc_ref[...]], axis=1)
    for sub in range(SWA_BLOCKS_PER_STEP):
        q_cols = slice(sub * blk, (sub + 1) * blk)
        k_band = k_all[sub * blk:(sub + 2) * blk, :]
        vt_band = vt_all[:, sub * blk:(sub + 2) * blk]
        if sub == 0:
            bias = bias_ref[jnp.minimum(pl.program_id(0), 1)]
        else:
            bias = bias_ref[1]
        for kvh in range(SWA_KVH):
            cols = [qt_ref[(kvh * SWA_GROUP + g) * hd:(kvh * SWA_GROUP + g + 1) * hd, q_cols]
                    for g in range(SWA_GROUP)]
            q_cat = jnp.concatenate(cols, axis=1)
            s_t = jnp.dot(k_band[:, kvh * hd:(kvh + 1) * hd], q_cat,
                          preferred_element_type=F32) + bias
            sink = sink_ref[kvh:kvh + 1, :] * LOG2_E
            m = jnp.maximum(jnp.max(s_t, axis=0, keepdims=True), sink)
            e = jnp.exp2(s_t - m)
            denom = jnp.sum(e, axis=0, keepdims=True) + jnp.exp2(sink - m)
            o_t = jnp.dot(vt_band[kvh * hd:(kvh + 1) * hd, :], e.astype(BF16),
                          preferred_element_type=F32)
            o_t = o_t * (1.0 / denom)
            for g in range(SWA_GROUP):
                r0 = (kvh * SWA_GROUP + g) * hd
                o_ref[r0:r0 + hd, q_cols] = o_t[:, g * blk:(g + 1) * blk].astype(o_ref.dtype)


def _swa_core(qt, k, vt, sinks, rounding_jobs):
    s = qt.shape[1]
    blk = SWA_BLOCK
    n = SWA_BLOCKS_PER_STEP
    w_in_specs, w_out_specs, w_out_shape = _rounding_specs(rounding_jobs, s // (n * blk))
    bias = jnp.asarray(_swa_bias())
    sink_rows = jnp.repeat(sinks.reshape(SWA_KVH, SWA_GROUP), blk, axis=1)

    def prev(i):
        return jnp.maximum(i * n - 1, 0)

    return pl.pallas_call(
        functools.partial(_swa_kernel, n_jobs=len(rounding_jobs)),
        grid=(s // (n * blk),),
        in_specs=[
            pl.BlockSpec((D_MODEL, n * blk), lambda i: (0, i)),
            pl.BlockSpec((blk, SWA_KVW), lambda i: (prev(i), 0)),
            pl.BlockSpec((n * blk, SWA_KVW), lambda i: (i, 0)),
            pl.BlockSpec((SWA_KVW, blk), lambda i: (0, prev(i))),
            pl.BlockSpec((SWA_KVW, n * blk), lambda i: (0, i)),
            pl.BlockSpec((SWA_KVH, SWA_GROUP * blk), lambda i: (0, 0)),
            pl.BlockSpec(bias.shape, lambda i: (0, 0, 0)),
        ] + w_in_specs,
        out_specs=[pl.BlockSpec((D_MODEL, n * blk), lambda i: (0, i))] + w_out_specs,
        out_shape=[jax.ShapeDtypeStruct((D_MODEL, s), BF16)] + w_out_shape,
        compiler_params=_params(1),
        name="swa_core",
    )(qt, k, k, vt, vt, sink_rows, bias, *[job[0] for job in rounding_jobs])


def kernel(x, norm_mix, norm_mlp, mlp_w1, mlp_w2, a_w_in, a_w_g2, a_b_g, a_g_o, a_w_o, kv_norm,
           kv_w_k, kv_w_v, kv_g_k, b_w_q, b_g_q, b_sinks, b_w_o):
    assert x.shape == (1, SEQ, D_MODEL)
    h = x[0]
    k_shared = vt_shared = None
    wq_next = None
    w_in_next = None
    a_w_in_t = jnp.swapaxes(a_w_in, 1, 2)
    for layer in range(DEPTH):
        jobs = [(mlp_w1, layer), (mlp_w2, layer)]
        if layer < N_GLA:
            if layer + 1 == N_GLA:
                jobs.append((b_w_q, 0))
            else:
                jobs.append((a_w_in_t, layer + 1, GLA_MAIN))
            proj, glr, wob = _gla_inproj(h, norm_mix[layer], a_w_in_t, a_w_o, layer, tm=TM,
                                         tn=TN if w_in_next is None else TN_WIDE,
                                         w_main=w_in_next)
            h, w1b, w2b, *rest = _gla_core(proj, glr, a_w_g2[layer], a_b_g[layer], a_g_o[layer],
                                           wob, h, jobs)
            if layer + 1 == N_GLA:
                (wq_next,) = rest
            else:
                (w_in_next,) = rest
        else:
            j = layer - N_GLA
            jobs.append((b_w_o, j))
            if layer + 1 < DEPTH:
                jobs.append((b_w_q, j + 1))
            kv = (kv_norm, kv_w_k, kv_w_v, kv_g_k) if k_shared is None else None
            qt, *shared = _swa_proj(h, norm_mix[layer], wq_next, b_g_q[j], tm=TM, tn=TN, kv=kv)
            if kv is not None:
                k_shared, vt_shared = shared
            ot, w1b, w2b, wob, *rest = _swa_core(qt, k_shared, vt_shared, b_sinks[j], jobs)
            h = _matmul_residual(ot, wob, h, tm=TM_OUT, a_transposed=True)
            if rest:
                (wq_next,) = rest
        h = _mlp(h, norm_mlp[layer], w1b, w2b, tm=TM, tf=TF)
    return h[None]
```

```python
import functools

import jax
import jax.numpy as jnp
import numpy as np
from jax import lax
from jax.experimental import pallas as pl
from jax.experimental.pallas import tpu as pltpu

F32 = jnp.float32
BF16 = jnp.bfloat16

EPS = 1e-6
LOG2_E = 1.4426950408889634
D_MODEL = 2048
SEQ = 8192
DEPTH = 4
N_GLA = DEPTH // 2
D_FF = 4 * D_MODEL

GLA_HEADS = 4
GLA_DK = 256
GLA_DV = 512
GLA_RANK = 16
GLA_TAU = 16.0
GLA_QK = GLA_HEADS * GLA_DK
GLA_VV = GLA_HEADS * GLA_DV
GLA_MAIN = 2 * GLA_QK + 2 * GLA_VV
GLA_CHUNK = 256
LANES = 128
BF16_ROWS_PER_VREG = 16
ROWS_PER_SLAB = 256

SWA_HD = 64
SWA_QH = 32
SWA_KVH = 4
SWA_GROUP = SWA_QH // SWA_KVH
SWA_WINDOW = 128
SWA_BLOCK = 128
SWA_KVW = SWA_KVH * SWA_HD
SWA_BLOCKS_PER_STEP = 4

V7X_VMEM_BYTES = 64 * 1024 * 1024
VMEM_LIMIT = V7X_VMEM_BYTES - 2 * 1024 * 1024

TM = 1024
TN = 1024
TN_WIDE = 2048
TF = 1024
TM_OUT = 512


def _params(n_axes):
    return pltpu.CompilerParams(
        dimension_semantics=("arbitrary",) * n_axes, vmem_limit_bytes=VMEM_LIMIT)


def _rmsnorm_rows(x, g):
    ms = jnp.mean(x * x, axis=-1, keepdims=True)
    return x * lax.rsqrt(ms + EPS) * g


def _dot_nt(a, b):
    return lax.dot_general(a, b, (((1,), (1,)), ((), ())), preferred_element_type=F32)


def _dot_tn(a, b):
    return lax.dot_general(a, b, (((0,), (0,)), ((), ())), preferred_element_type=F32)


def _row_slabs(n_rows):
    return [slice(r, r + ROWS_PER_SLAB) for r in range(0, n_rows, ROWS_PER_SLAB)]


def _x_tile_copy(x_hbm, x_buf, sem, tile):
    rows = x_buf.shape[0]
    start = pl.multiple_of(tile * rows, rows)
    return pltpu.make_async_copy(x_hbm.at[pl.ds(start, rows), :], x_buf, sem.at[0])


def _stage_x_tile(x_hbm, x_buf, sem):
    i, j = pl.program_id(0), pl.program_id(1)

    @pl.when((i == 0) & (j == 0))
    def _():
        _x_tile_copy(x_hbm, x_buf, sem, 0).start()

    @pl.when(j == 0)
    def _():
        _x_tile_copy(x_hbm, x_buf, sem, i).wait()

    @pl.when((j == 1) & (i + 1 < pl.num_programs(0)))
    def _():
        _x_tile_copy(x_hbm, x_buf, sem, i + 1).start()


def _x_stage_scratch(tm, d, n_column_steps):
    assert n_column_steps >= 2
    return [pltpu.VMEM((tm, d), F32), pltpu.SemaphoreType.DMA((1,))]


def _dot_wt(w, hn):
    return lax.dot_general(w, hn, (((0,), (1,)), ((), ())), preferred_element_type=F32)


def _gla_inproj_kernel(x_hbm, g_ref, wt_ref, wgt_ref, wo_ref, o_ref, glr_ref, wob_ref, hn_ref,
                       x_ref, x_sem, *, first_gate_tile):
    _stage_x_tile(x_hbm, x_ref, x_sem)

    @pl.when(pl.program_id(1) == 0)
    def _():
        wob_ref[...] = wo_ref[...].astype(BF16)
        w = wt_ref[...].astype(BF16)
        wg = wgt_ref[...]
        row = lax.broadcasted_iota(jnp.int32, wg.shape, 0)
        wg = jnp.where(row < GLA_RANK, wg, 0.0).astype(BF16)
        for sl in _row_slabs(x_ref.shape[0]):
            hn = _rmsnorm_rows(x_ref[sl, :], g_ref[...]).astype(BF16)
            hn_ref[sl, :] = hn
            glr_ref[sl, :] = _dot_nt(hn, wg)
            o_ref[sl, :] = _dot_nt(hn, w).astype(o_ref.dtype)

    is_gate_tile = pl.program_id(1) >= first_gate_tile

    @pl.when((pl.program_id(1) != 0) & jnp.logical_not(is_gate_tile))
    def _():
        y = _dot_nt(hn_ref[...], wt_ref[...].astype(BF16))
        o_ref[...] = y.astype(o_ref.dtype)

    @pl.when(is_gate_tile)
    def _():
        y = _dot_nt(hn_ref[...], wt_ref[...].astype(BF16))
        o_ref[...] = (y * jax.nn.sigmoid(y)).astype(o_ref.dtype)


def _gla_inproj(x, g, wt, w_o, layer, tm, tn, w_main=None):
    s, d = x.shape
    wo_rows = w_o.shape[1] // (s // tm)
    assert w_o.shape[1] % (s // tm) == 0 and wo_rows % BF16_ROWS_PER_VREG == 0
    main_spec = (pl.BlockSpec((None, tn, d), lambda i, j: (layer, j, 0)) if w_main is None
                 else pl.BlockSpec((tn, d), lambda i, j: (j, 0)))
    first_gate_tile = (2 * GLA_QK + GLA_VV) // tn
    assert first_gate_tile >= 1 and (2 * GLA_QK + GLA_VV) % tn == 0
    return pl.pallas_call(
        functools.partial(_gla_inproj_kernel, first_gate_tile=first_gate_tile),
        grid=(s // tm, GLA_MAIN // tn),
        in_specs=[
            pl.BlockSpec(memory_space=pl.ANY),
            pl.BlockSpec((1, d), lambda i, j: (0, 0)),
            main_spec,
            pl.BlockSpec((None, LANES, d), lambda i, j: (layer, GLA_MAIN // LANES, 0)),
            pl.BlockSpec((None, wo_rows, w_o.shape[2]), lambda i, j: (layer, i, 0)),
        ],
        out_specs=[
            pl.BlockSpec((tm, tn), lambda i, j: (i, j)),
            pl.BlockSpec((tm, LANES), lambda i, j: (i, 0)),
            pl.BlockSpec((wo_rows, w_o.shape[2]), lambda i, j: (i, 0)),
        ],
        out_shape=[
            jax.ShapeDtypeStruct((s, GLA_MAIN), BF16),
            jax.ShapeDtypeStruct((s, LANES), F32),
            jax.ShapeDtypeStruct(w_o.shape[1:], BF16),
        ],
        scratch_shapes=[pltpu.VMEM((tm, d), BF16)] + _x_stage_scratch(tm, d, GLA_MAIN // tn),
        compiler_params=_params(2),
        name="gla_inproj",
    )(x, g.reshape(1, d), wt if w_main is None else w_main, wt, w_o)


def _swa_proj_kernel(*refs, with_kv):
    if with_kv:
        (x_hbm, gq_ref, wq_ref, ghq_ref, gkv_ref, wk_ref, wv_ref, gk_ref,
         qt_ref, k_ref, vt_ref, hnq_ref, x_ref, x_sem) = refs
    else:
        x_hbm, gq_ref, wq_ref, ghq_ref, qt_ref, hnq_ref, x_ref, x_sem = refs
    _stage_x_tile(x_hbm, x_ref, x_sem)

    def head_normed(q_t):
        heads = []
        for r0 in range(0, q_t.shape[0], SWA_HD):
            q_h = q_t[r0:r0 + SWA_HD, :]
            ms = jnp.mean(q_h * q_h, axis=0, keepdims=True)
            scale = lax.rsqrt(ms + EPS) * (SWA_HD ** -0.5 * LOG2_E)
            heads.append(q_h * scale * ghq_ref[...])
        return jnp.concatenate(heads, axis=0).astype(qt_ref.dtype)

    @pl.when(pl.program_id(1) == 0)
    def _():
        wq = wq_ref[...].astype(BF16)
        if with_kv:
            wk = wk_ref[...].astype(BF16)
            wv = wv_ref[...].astype(BF16)
        for sl in _row_slabs(x_ref.shape[0]):
            x = x_ref[sl, :]
            y = x * lax.rsqrt(jnp.mean(x * x, axis=-1, keepdims=True) + EPS)
            hn_q = (y * gq_ref[...]).astype(BF16)
            hnq_ref[sl, :] = hn_q
            qt_ref[:, sl] = head_normed(_dot_wt(wq, hn_q))
            if with_kv:
                hn_kv = (y * gkv_ref[...]).astype(BF16)
                k = jnp.dot(hn_kv, wk, preferred_element_type=F32)
                heads = [_rmsnorm_rows(k[:, h * SWA_HD:(h + 1) * SWA_HD], gk_ref[...])
                         for h in range(SWA_KVH)]
                k_ref[sl, :] = jnp.concatenate(heads, axis=1).astype(k_ref.dtype)
                vt_ref[:, sl] = _dot_wt(wv, hn_kv).astype(vt_ref.dtype)

    @pl.when(pl.program_id(1) != 0)
    def _():
        qt_ref[...] = head_normed(_dot_wt(wq_ref[...].astype(BF16), hnq_ref[...]))


def _swa_proj(x, g_q, w_q, g_hq, tm, tn, kv=None):
    s, d = x.shape
    n = w_q.shape[1]
    in_specs = [
        pl.BlockSpec(memory_space=pl.ANY),
        pl.BlockSpec((1, d), lambda i, j: (0, 0)),
        pl.BlockSpec((d, tn), lambda i, j: (0, j)),
        pl.BlockSpec((SWA_HD, 1), lambda i, j: (0, 0)),
    ]
    args = [x, g_q.reshape(1, d), w_q, g_hq.reshape(SWA_HD, 1)]
    out_specs = [pl.BlockSpec((tn, tm), lambda i, j: (j, i))]
    out_shape = [jax.ShapeDtypeStruct((n, s), BF16)]
    scratch = [pltpu.VMEM((tm, d), BF16)] + _x_stage_scratch(tm, d, n // tn)
    if kv is not None:
        g_kv, w_k, w_v, g_k = kv
        in_specs += [
            pl.BlockSpec((1, d), lambda i, j: (0, 0)),
            pl.BlockSpec((d, SWA_KVW), lambda i, j: (0, 0)),
            pl.BlockSpec((d, SWA_KVW), lambda i, j: (0, 0)),
            pl.BlockSpec((1, SWA_HD), lambda i, j: (0, 0)),
        ]
        args += [g_kv.reshape(1, d), w_k, w_v, g_k.reshape(1, SWA_HD)]
        out_specs += [
            pl.BlockSpec((tm, SWA_KVW), lambda i, j: (i, 0)),
            pl.BlockSpec((SWA_KVW, tm), lambda i, j: (0, i)),
        ]
        out_shape += [
            jax.ShapeDtypeStruct((s, SWA_KVW), BF16),
            jax.ShapeDtypeStruct((SWA_KVW, s), BF16),
        ]
    return pl.pallas_call(
        functools.partial(_swa_proj_kernel, with_kv=kv is not None),
        grid=(s // tm, n // tn),
        in_specs=in_specs,
        out_specs=out_specs,
        out_shape=out_shape,
        scratch_shapes=scratch,
        compiler_params=_params(2),
        name="swa_proj",
    )(*args)


def _matmul_residual_kernel(a_ref, w_ref, r_ref, o_ref, *, a_transposed):
    if a_transposed:
        y = _dot_tn(a_ref[...], w_ref[...])
    else:
        y = jnp.dot(a_ref[...], w_ref[...], preferred_element_type=F32)
    o_ref[...] = r_ref[...] + y


def _matmul_residual(a, w, res, tm, a_transposed=False):
    k, n = w.shape
    s = res.shape[0]
    a_spec = (pl.BlockSpec((k, tm), lambda i: (0, i)) if a_transposed
              else pl.BlockSpec((tm, k), lambda i: (i, 0)))
    return pl.pallas_call(
        functools.partial(_matmul_residual_kernel, a_transposed=a_transposed),
        grid=(s // tm,),
        in_specs=[
            a_spec,
            pl.BlockSpec((k, n), lambda i: (0, 0)),
            pl.BlockSpec((tm, n), lambda i: (i, 0)),
        ],
        out_specs=pl.BlockSpec((tm, n), lambda i: (i, 0)),
        out_shape=jax.ShapeDtypeStruct((s, n), F32),
        compiler_params=_params(1),
        name="matmul_residual",
    )(a, w, res)


def _mlp_kernel(x_hbm, g_ref, w1_ref, w2_ref, o_ref, hn_ref, x_ref, x_sem):
    _stage_x_tile(x_hbm, x_ref, x_sem)

    def hidden_out(hn, w1, w2):
        u = jnp.maximum(jnp.dot(hn, w1, preferred_element_type=F32), 0.0)
        return jnp.dot((u * u).astype(BF16), w2, preferred_element_type=F32)

    @pl.when(pl.program_id(1) == 0)
    def _():
        w1 = w1_ref[...].astype(BF16)
        w2 = w2_ref[...].astype(BF16)
        for sl in _row_slabs(x_ref.shape[0]):
            x = x_ref[sl, :]
            hn = _rmsnorm_rows(x, g_ref[...]).astype(BF16)
            hn_ref[sl, :] = hn
            o_ref[sl, :] = x + hidden_out(hn, w1, w2)

    @pl.when(pl.program_id(1) != 0)
    def _():
        o_ref[...] += hidden_out(hn_ref[...], w1_ref[...].astype(BF16), w2_ref[...].astype(BF16))


def _mlp(x, g, w1, w2, tm, tf):
    s, d = x.shape
    f = w1.shape[1]
    return pl.pallas_call(
        _mlp_kernel,
        grid=(s // tm, f // tf),
        in_specs=[
            pl.BlockSpec(memory_space=pl.ANY),
            pl.BlockSpec((1, d), lambda i, j: (0, 0)),
            pl.BlockSpec((d, tf), lambda i, j: (0, j)),
            pl.BlockSpec((tf, d), lambda i, j: (j, 0)),
        ],
        out_specs=pl.BlockSpec((tm, d), lambda i, j: (i, 0)),
        out_shape=jax.ShapeDtypeStruct((s, d), F32),
        scratch_shapes=[pltpu.VMEM((tm, d), BF16)] + _x_stage_scratch(tm, d, f // tf),
        compiler_params=_params(2),
        name="mlp",
    )(x, g.reshape(1, d), w1, w2)


def _rounding_specs(jobs, n_steps):
    in_specs, out_specs, out_shape = [], [], []
    for w, layer, *limit in jobs:
        _, rows, cols = w.shape
        if limit:
            (rows,) = limit
        slab = rows // n_steps
        assert rows % n_steps == 0 and slab % BF16_ROWS_PER_VREG == 0
        in_specs.append(pl.BlockSpec((None, slab, cols), lambda i, layer=layer: (layer, i, 0)))
        out_specs.append(pl.BlockSpec((slab, cols), lambda i: (i, 0)))
        out_shape.append(jax.ShapeDtypeStruct((rows, cols), BF16))
    return in_specs, out_specs, out_shape


def _round_slabs(src_refs, dst_refs):
    for src, dst in zip(src_refs, dst_refs):
        dst[...] = src[...].astype(BF16)


@functools.lru_cache(maxsize=None)
def _gla_constants(c):
    i = np.arange(c)[:, None]
    t = np.arange(c)[None, :]
    mats = [t <= i, t > i]
    masks = []
    s = c // 2
    while s >= 1:
        pos = i % (2 * s)
        ref_row = i - pos + s
        upper = pos >= s
        mats.append(np.where(upper, (t > ref_row) & (t <= i), (t > i) & (t <= ref_row)))
        same_block = (i // (2 * s)) == (t // (2 * s))
        masks.append(same_block & upper & ((t % (2 * s)) < s))
        s //= 2
    masks.append(i == t)
    sums = np.concatenate(mats, axis=0).astype(np.float32)
    return sums, np.stack(masks).astype(np.float32)


def _gla_kernel(q_ref, k_ref, v_ref, r_ref, glr_ref, wg_ref, bg_ref, go_ref, sums_ref, mask_ref,
                wo_ref, res_ref, *refs, n_jobs):
    h_ref, st_ref = refs[n_jobs], refs[-1]
    _round_slabs(refs[:n_jobs], refs[n_jobs + 1:-1])
    gated = []
    c = GLA_CHUNK
    n_levels = mask_ref.shape[0] - 1

    @pl.when(pl.program_id(0) == 0)
    def _():
        st_ref[...] = jnp.zeros_like(st_ref)

    gate_in = jnp.dot(glr_ref[...].astype(BF16), wg_ref[...], preferred_element_type=F32)
    gate_in = gate_in + bg_ref[...]
    log_sig = jnp.minimum(gate_in, 0.0) - jnp.log1p(jnp.exp(-jnp.abs(gate_in)))
    log_a = log_sig * (LOG2_E / GLA_TAU)

    la_hi = log_a.astype(BF16)
    la_lo = (log_a - la_hi.astype(F32)).astype(BF16)
    chunk_sums = sums_ref[0:2 * c, :]
    part_chunk = (jnp.dot(chunk_sums, la_hi, preferred_element_type=F32)
                  + jnp.dot(chunk_sums, la_lo, preferred_element_type=F32))
    part_level = jnp.dot(sums_ref[2 * c:, :], la_hi, preferred_element_type=F32)

    for h in range(GLA_HEADS):
        qk_cols = slice(h * GLA_DK, (h + 1) * GLA_DK)
        v_cols = slice(h * GLA_DV, (h + 1) * GLA_DV)
        q = q_ref[:, qk_cols] * (GLA_DK ** -0.5)
        k = k_ref[:, qk_cols]
        v = v_ref[:, v_cols]

        b = part_chunk[0:c, qk_cols]
        state_t = st_ref[h]
        o = _dot_nt(q * jnp.exp2(b).astype(BF16), state_t.astype(BF16))

        attn = _dot_nt(q, k).astype(BF16) * mask_ref[n_levels]
        for lvl in range(n_levels):
            e = jnp.exp2(part_level[lvl * c:(lvl + 1) * c, qk_cols]).astype(BF16)
            attn = attn + _dot_nt(q * e, k * e).astype(BF16) * mask_ref[lvl]
        o = o + jnp.dot(attn, v, preferred_element_type=F32)

        k_dec = k * jnp.exp2(part_chunk[c:2 * c, qk_cols]).astype(BF16)
        st_ref[h] = state_t * jnp.exp2(b[c - 1:c, :]) + _dot_tn(v, k_dec)

        o = _rmsnorm_rows(o, go_ref[...])
        gated.append((o * r_ref[:, v_cols].astype(F32)).astype(BF16))

    h_ref[...] = res_ref[...] + jnp.dot(jnp.concatenate(gated, axis=1), wo_ref[...],
                                        preferred_element_type=F32)


def _gla_core(proj, glr, w_g2, b_g, g_o, w_o, res, rounding_jobs):
    s = proj.shape[0]
    c = GLA_CHUNK
    w_in_specs, w_out_specs, w_out_shape = _rounding_specs(rounding_jobs, s // c)
    sums, masks = _gla_constants(c)
    sums = jnp.asarray(sums, dtype=BF16)
    masks = jnp.asarray(masks, dtype=BF16)
    w_g2p = jnp.pad(w_g2, ((0, LANES - GLA_RANK), (0, 0))).astype(BF16)
    assert 2 * GLA_QK == GLA_VV
    return pl.pallas_call(
        functools.partial(_gla_kernel, n_jobs=len(rounding_jobs)),
        grid=(s // c,),
        in_specs=[
            pl.BlockSpec((c, GLA_QK), lambda i: (i, 0)),
            pl.BlockSpec((c, GLA_QK), lambda i: (i, 1)),
            pl.BlockSpec((c, GLA_VV), lambda i: (i, 1)),
            pl.BlockSpec((c, GLA_VV), lambda i: (i, 2)),
            pl.BlockSpec((c, LANES), lambda i: (i, 0)),
            pl.BlockSpec((LANES, GLA_QK), lambda i: (0, 0)),
            pl.BlockSpec((1, GLA_QK), lambda i: (0, 0)),
            pl.BlockSpec((1, GLA_DV), lambda i: (0, 0)),
            pl.BlockSpec(sums.shape, lambda i: (0, 0)),
            pl.BlockSpec(masks.shape, lambda i: (0, 0, 0)),
            pl.BlockSpec(w_o.shape, lambda i: (0, 0)),
            pl.BlockSpec((c, D_MODEL), lambda i: (i, 0)),
        ] + w_in_specs,
        out_specs=[pl.BlockSpec((c, D_MODEL), lambda i: (i, 0))] + w_out_specs,
        out_shape=[jax.ShapeDtypeStruct((s, D_MODEL), F32)] + w_out_shape,
        scratch_shapes=[pltpu.VMEM((GLA_HEADS, GLA_DV, GLA_DK), F32)],
        compiler_params=_params(1),
        name="gla_core",
    )(proj, proj, proj, proj, glr, w_g2p, b_g.reshape(1, GLA_QK), g_o.reshape(1, GLA_DV),
      sums, masks, w_o, res, *[job[0] for job in rounding_jobs])


@functools.lru_cache(maxsize=None)
def _swa_bias():
    blk = SWA_BLOCK
    kj = np.arange(2 * blk)[:, None]
    qi = np.arange(blk)[None, :]
    rel = qi + blk - kj
    valid = (rel >= 0) & (rel < SWA_WINDOW)
    first = valid & (kj >= blk)
    planes = np.stack([first, valid])
    bias = np.where(planes, 0.0, -np.inf).astype(np.float32)
    return np.tile(bias, (1, 1, SWA_GROUP))


def _swa_kernel(qt_ref, kp_ref, kc_ref, vtp_ref, vtc_ref, sink_ref, bias_ref, *refs, n_jobs):
    o_ref = refs[n_jobs]
    _round_slabs(refs[:n_jobs], refs[n_jobs + 1:])
    hd = SWA_HD
    blk = SWA_BLOCK
    k_all = jnp.concatenate([kp_ref[...], kc_ref[...]], axis=0)
    vt_all = jnp.concatenate([vtp_ref[...], vtc_ref[...]], axis=1)
    for sub in range(SWA_BLOCKS_PER_STEP):
        q_cols = slice(sub * blk, (sub + 1) * blk)
        k_band = k_all[sub * blk:(sub + 2) * blk, :]
        vt_band = vt_all[:, sub * blk:(sub + 2) * blk]
        if sub == 0:
            bias = bias_ref[jnp.minimum(pl.program_id(0), 1)]
        else:
            bias = bias_ref[1]
        for kvh in range(SWA_KVH):
            cols = [qt_ref[(kvh * SWA_GROUP + g) * hd:(kvh * SWA_GROUP + g + 1) * hd, q_cols]
                    for g in range(SWA_GROUP)]
            q_cat = jnp.concatenate(cols, axis=1)
            s_t = jnp.dot(k_band[:, kvh * hd:(kvh + 1) * hd], q_cat,
                          preferred_element_type=F32) + bias
            sink = sink_ref[kvh:kvh + 1, :] * LOG2_E
            m = jnp.maximum(jnp.max(s_t, axis=0, keepdims=True), sink)
            e = jnp.exp2(s_t - m)
            denom = jnp.sum(e, axis=0, keepdims=True) + jnp.exp2(sink - m)
            o_t = jnp.dot(vt_band[kvh * hd:(kvh + 1) * hd, :], e.astype(BF16),
                          preferred_element_type=F32)
            o_t = o_t * (1.0 / denom)
            for g in range(SWA_GROUP):
                r0 = (kvh * SWA_GROUP + g) * hd
                o_ref[r0:r0 + hd, q_cols] = o_t[:, g * blk:(g + 1) * blk].astype(o_ref.dtype)


def _swa_core(qt, k, vt, sinks, rounding_jobs):
    s = qt.shape[1]
    blk = SWA_BLOCK
    n = SWA_BLOCKS_PER_STEP
    w_in_specs, w_out_specs, w_out_shape = _rounding_specs(rounding_jobs, s // (n * blk))
    bias = jnp.asarray(_swa_bias())
    sink_rows = jnp.repeat(sinks.reshape(SWA_KVH, SWA_GROUP), blk, axis=1)

    def prev(i):
        return jnp.maximum(i * n - 1, 0)

    return pl.pallas_call(
        functools.partial(_swa_kernel, n_jobs=len(rounding_jobs)),
        grid=(s // (n * blk),),
        in_specs=[
            pl.BlockSpec((D_MODEL, n * blk), lambda i: (0, i)),
            pl.BlockSpec((blk, SWA_KVW), lambda i: (prev(i), 0)),
            pl.BlockSpec((n * blk, SWA_KVW), lambda i: (i, 0)),
            pl.BlockSpec((SWA_KVW, blk), lambda i: (0, prev(i))),
            pl.BlockSpec((SWA_KVW, n * blk), lambda i: (0, i)),
            pl.BlockSpec((SWA_KVH, SWA_GROUP * blk), lambda i: (0, 0)),
            pl.BlockSpec(bias.shape, lambda i: (0, 0, 0)),
        ] + w_in_specs,
        out_specs=[pl.BlockSpec((D_MODEL, n * blk), lambda i: (0, i))] + w_out_specs,
        out_shape=[jax.ShapeDtypeStruct((D_MODEL, s), BF16)] + w_out_shape,
        compiler_params=_params(1),
        name="swa_core",
    )(qt, k, k, vt, vt, sink_rows, bias, *[job[0] for job in rounding_jobs])


def kernel(x, norm_mix, norm_mlp, mlp_w1, mlp_w2, a_w_in, a_w_g2, a_b_g, a_g_o, a_w_o, kv_norm,
           kv_w_k, kv_w_v, kv_g_k, b_w_q, b_g_q, b_sinks, b_w_o):
    assert x.shape == (1, SEQ, D_MODEL)
    h = x[0]
    k_shared = vt_shared = None
    wq_next = None
    w_in_next = None
    a_w_in_t = jnp.swapaxes(a_w_in, 1, 2)
    for layer in range(DEPTH):
        jobs = [(mlp_w1, layer), (mlp_w2, layer)]
        if layer < N_GLA:
            if layer + 1 == N_GLA:
                jobs.append((b_w_q, 0))
            else:
                jobs.append((a_w_in_t, layer + 1, GLA_MAIN))
            proj, glr, wob = _gla_inproj(h, norm_mix[layer], a_w_in_t, a_w_o, layer, tm=TM,
                                         tn=TN if w_in_next is None else TN_WIDE,
                                         w_main=w_in_next)
            h, w1b, w2b, *rest = _gla_core(proj, glr, a_w_g2[layer], a_b_g[layer], a_g_o[layer],
                                           wob, h, jobs)
            if layer + 1 == N_GLA:
                (wq_next,) = rest
            else:
                (w_in_next,) = rest
        else:
            j = layer - N_GLA
            jobs.append((b_w_o, j))
            if layer + 1 < DEPTH:
                jobs.append((b_w_q, j + 1))
            kv = (kv_norm, kv_w_k, kv_w_v, kv_g_k) if k_shared is None else None
            qt, *shared = _swa_proj(h, norm_mix[layer], wq_next, b_g_q[j], tm=TM, tn=TN, kv=kv)
            if kv is not None:
                k_shared, vt_shared = shared
            ot, w1b, w2b, wob, *rest = _swa_core(qt, k_shared, vt_shared, b_sinks[j], jobs)
            h = _matmul_residual(ot, wob, h, tm=TM_OUT, a_transposed=True)
            if rest:
                (wq_next,) = rest
        h = _mlp(h, norm_mlp[layer], w1b, w2b, tm=TM, tf=TF)
    return h[None]
```

```python
import functools

import jax
import jax.numpy as jnp
import numpy as np
from jax import lax
from jax.experimental import pallas as pl
from jax.experimental.pallas import tpu as pltpu

F32 = jnp.float32
BF16 = jnp.bfloat16

EPS = 1e-6
LOG2_E = 1.4426950408889634
D_MODEL = 2048
SEQ = 8192
DEPTH = 4
N_GLA = DEPTH // 2
D_FF = 4 * D_MODEL

GLA_HEADS = 4
GLA_DK = 256
GLA_DV = 512
GLA_RANK = 16
GLA_TAU = 16.0
GLA_QK = GLA_HEADS * GLA_DK
GLA_VV = GLA_HEADS * GLA_DV
GLA_MAIN = 2 * GLA_QK + 2 * GLA_VV
GLA_CHUNK = 256
LANES = 128
BF16_ROWS_PER_VREG = 16
ROWS_PER_SLAB = 256

SWA_HD = 64
SWA_QH = 32
SWA_KVH = 4
SWA_GROUP = SWA_QH // SWA_KVH
SWA_WINDOW = 128
SWA_BLOCK = 128
SWA_KVW = SWA_KVH * SWA_HD
SWA_BLOCKS_PER_STEP = 4

V7X_VMEM_BYTES = 64 * 1024 * 1024
VMEM_LIMIT = V7X_VMEM_BYTES - 2 * 1024 * 1024

TM = 1024
TN = 1024
TN_WIDE = 2048
TF = 1024
WEIGHT_RING_SLOTS = 3
TM_OUT = 512


def _params(n_axes):
    return pltpu.CompilerParams(
        dimension_semantics=("arbitrary",) * n_axes, vmem_limit_bytes=VMEM_LIMIT)


def _rmsnorm_rows(x, g):
    ms = jnp.mean(x * x, axis=-1, keepdims=True)
    return x * lax.rsqrt(ms + EPS) * g


def _dot_nt(a, b):
    return lax.dot_general(a, b, (((1,), (1,)), ((), ())), preferred_element_type=F32)


def _dot_tn(a, b):
    return lax.dot_general(a, b, (((0,), (0,)), ((), ())), preferred_element_type=F32)


def _row_slabs(n_rows):
    return [slice(r, r + ROWS_PER_SLAB) for r in range(0, n_rows, ROWS_PER_SLAB)]


def _x_tile_copy(x_hbm, x_buf, sem, tile):
    rows = x_buf.shape[0]
    start = pl.multiple_of(tile * rows, rows)
    return pltpu.make_async_copy(x_hbm.at[pl.ds(start, rows), :], x_buf, sem.at[0])


def _stage_x_tile(x_hbm, x_buf, sem):
    i, j = pl.program_id(0), pl.program_id(1)

    @pl.when((i == 0) & (j == 0))
    def _():
        _x_tile_copy(x_hbm, x_buf, sem, 0).start()

    @pl.when(j == 0)
    def _():
        _x_tile_copy(x_hbm, x_buf, sem, i).wait()

    @pl.when((j == 1) & (i + 1 < pl.num_programs(0)))
    def _():
        _x_tile_copy(x_hbm, x_buf, sem, i + 1).start()


def _x_stage_scratch(tm, d, n_column_steps):
    assert n_column_steps >= 2
    return [pltpu.VMEM((tm, d), F32), pltpu.SemaphoreType.DMA((1,))]


def _dot_wt(w, hn):
    return lax.dot_general(w, hn, (((0,), (1,)), ((), ())), preferred_element_type=F32)


def _weight_ring_tile(wt_hbm, w_buf, sem, layer):
    n_col = pl.num_programs(1)
    n_steps = pl.num_programs(0) * n_col
    step = pl.program_id(0) * n_col + pl.program_id(1)
    tn = w_buf.shape[1]

    def copy(t):
        slot = lax.rem(t, WEIGHT_RING_SLOTS)
        start = pl.multiple_of(lax.rem(t, n_col) * tn, tn)
        return pltpu.make_async_copy(wt_hbm.at[layer, pl.ds(start, tn), :], w_buf.at[slot],
                                     sem.at[slot])

    @pl.when(step == 0)
    def _():
        copy(0).start()
        copy(1).start()

    @pl.when(step + 2 < n_steps)
    def _():
        copy(step + 2).start()

    copy(step).wait()
    return w_buf.at[lax.rem(step, WEIGHT_RING_SLOTS)]


def _gla_inproj_kernel(x_hbm, g_ref, wt_ref, wgt_ref, wo_ref, o_ref, glr_ref, wob_ref, hn_ref,
                       x_ref, x_sem, *ring, first_gate_tile, ring_layer):
    _stage_x_tile(x_hbm, x_ref, x_sem)
    if ring:
        wt_ref = _weight_ring_tile(wt_ref, *ring, ring_layer)

    @pl.when(pl.program_id(1) == 0)
    def _():
        wob_ref[...] = wo_ref[...].astype(BF16)
        w = wt_ref[...].astype(BF16)
        wg = wgt_ref[...]
        row = lax.broadcasted_iota(jnp.int32, wg.shape, 0)
        wg = jnp.where(row < GLA_RANK, wg, 0.0).astype(BF16)
        for sl in _row_slabs(x_ref.shape[0]):
            hn = _rmsnorm_rows(x_ref[sl, :], g_ref[...]).astype(BF16)
            hn_ref[sl, :] = hn
            glr_ref[sl, :] = _dot_nt(hn, wg)
            o_ref[sl, :] = _dot_nt(hn, w).astype(o_ref.dtype)

    is_gate_tile = pl.program_id(1) >= first_gate_tile

    @pl.when((pl.program_id(1) != 0) & jnp.logical_not(is_gate_tile))
    def _():
        y = _dot_nt(hn_ref[...], wt_ref[...].astype(BF16))
        o_ref[...] = y.astype(o_ref.dtype)

    @pl.when(is_gate_tile)
    def _():
        y = _dot_nt(hn_ref[...], wt_ref[...].astype(BF16))
        o_ref[...] = (y * jax.nn.sigmoid(y)).astype(o_ref.dtype)


def _gla_inproj(x, g, wt, w_o, layer, tm, tn, w_main=None):
    s, d = x.shape
    wo_rows = w_o.shape[1] // (s // tm)
    assert w_o.shape[1] % (s // tm) == 0 and wo_rows % BF16_ROWS_PER_VREG == 0
    scratch = [pltpu.VMEM((tm, d), BF16)] + _x_stage_scratch(tm, d, GLA_MAIN // tn)
    if w_main is None:
        assert (s // tm) * (GLA_MAIN // tn) >= WEIGHT_RING_SLOTS
        main_spec = pl.BlockSpec(memory_space=pl.ANY)
        scratch += [pltpu.VMEM((WEIGHT_RING_SLOTS, tn, d), F32),
                    pltpu.SemaphoreType.DMA((WEIGHT_RING_SLOTS,))]
    else:
        main_spec = pl.BlockSpec((tn, d), lambda i, j: (j, 0))
    first_gate_tile = (2 * GLA_QK + GLA_VV) // tn
    assert first_gate_tile >= 1 and (2 * GLA_QK + GLA_VV) % tn == 0
    return pl.pallas_call(
        functools.partial(_gla_inproj_kernel, first_gate_tile=first_gate_tile, ring_layer=layer),
        grid=(s // tm, GLA_MAIN // tn),
        in_specs=[
            pl.BlockSpec(memory_space=pl.ANY),
            pl.BlockSpec((1, d), lambda i, j: (0, 0)),
            main_spec,
            pl.BlockSpec((None, LANES, d), lambda i, j: (layer, GLA_MAIN // LANES, 0)),
            pl.BlockSpec((None, wo_rows, w_o.shape[2]), lambda i, j: (layer, i, 0)),
        ],
        out_specs=[
            pl.BlockSpec((tm, tn), lambda i, j: (i, j)),
            pl.BlockSpec((tm, LANES), lambda i, j: (i, 0)),
            pl.BlockSpec((wo_rows, w_o.shape[2]), lambda i, j: (i, 0)),
        ],
        out_shape=[
            jax.ShapeDtypeStruct((s, GLA_MAIN), BF16),
            jax.ShapeDtypeStruct((s, LANES), F32),
            jax.ShapeDtypeStruct(w_o.shape[1:], BF16),
        ],
        scratch_shapes=scratch,
        compiler_params=_params(2),
        name="gla_inproj",
    )(x, g.reshape(1, d), wt if w_main is None else w_main, wt, w_o)


def _swa_proj_kernel(*refs, with_kv):
    if with_kv:
        (x_hbm, gq_ref, wq_ref, ghq_ref, gkv_ref, wk_ref, wv_ref, gk_ref,
         qt_ref, k_ref, vt_ref, hnq_ref, x_ref, x_sem) = refs
    else:
        x_hbm, gq_ref, wq_ref, ghq_ref, qt_ref, hnq_ref, x_ref, x_sem = refs
    _stage_x_tile(x_hbm, x_ref, x_sem)

    def head_normed(q_t):
        heads = []
        for r0 in range(0, q_t.shape[0], SWA_HD):
            q_h = q_t[r0:r0 + SWA_HD, :]
            ms = jnp.mean(q_h * q_h, axis=0, keepdims=True)
            scale = lax.rsqrt(ms + EPS) * (SWA_HD ** -0.5 * LOG2_E)
            heads.append(q_h * scale * ghq_ref[...])
        return jnp.concatenate(heads, axis=0).astype(qt_ref.dtype)

    @pl.when(pl.program_id(1) == 0)
    def _():
        wq = wq_ref[...].astype(BF16)
        if with_kv:
            wk = wk_ref[...].astype(BF16)
            wv = wv_ref[...].astype(BF16)
        for sl in _row_slabs(x_ref.shape[0]):
            x = x_ref[sl, :]
            y = x * lax.rsqrt(jnp.mean(x * x, axis=-1, keepdims=True) + EPS)
            hn_q = (y * gq_ref[...]).astype(BF16)
            hnq_ref[sl, :] = hn_q
            qt_ref[:, sl] = head_normed(_dot_wt(wq, hn_q))
            if with_kv:
                hn_kv = (y * gkv_ref[...]).astype(BF16)
                k = jnp.dot(hn_kv, wk, preferred_element_type=F32)
                heads = [_rmsnorm_rows(k[:, h * SWA_HD:(h + 1) * SWA_HD], gk_ref[...])
                         for h in range(SWA_KVH)]
                k_ref[sl, :] = jnp.concatenate(heads, axis=1).astype(k_ref.dtype)
                vt_ref[:, sl] = _dot_wt(wv, hn_kv).astype(vt_ref.dtype)

    @pl.when(pl.program_id(1) != 0)
    def _():
        qt_ref[...] = head_normed(_dot_wt(wq_ref[...].astype(BF16), hnq_ref[...]))


def _swa_proj(x, g_q, w_q, g_hq, tm, tn, kv=None):
    s, d = x.shape
    n = w_q.shape[1]
    in_specs = [
        pl.BlockSpec(memory_space=pl.ANY),
        pl.BlockSpec((1, d), lambda i, j: (0, 0)),
        pl.BlockSpec((d, tn), lambda i, j: (0, j)),
        pl.BlockSpec((SWA_HD, 1), lambda i, j: (0, 0)),
    ]
    args = [x, g_q.reshape(1, d), w_q, g_hq.reshape(SWA_HD, 1)]
    out_specs = [pl.BlockSpec((tn, tm), lambda i, j: (j, i))]
    out_shape = [jax.ShapeDtypeStruct((n, s), BF16)]
    scratch = [pltpu.VMEM((tm, d), BF16)] + _x_stage_scratch(tm, d, n // tn)
    if kv is not None:
        g_kv, w_k, w_v, g_k = kv
        in_specs += [
            pl.BlockSpec((1, d), lambda i, j: (0, 0)),
            pl.BlockSpec((d, SWA_KVW), lambda i, j: (0, 0)),
            pl.BlockSpec((d, SWA_KVW), lambda i, j: (0, 0)),
            pl.BlockSpec((1, SWA_HD), lambda i, j: (0, 0)),
        ]
        args += [g_kv.reshape(1, d), w_k, w_v, g_k.reshape(1, SWA_HD)]
        out_specs += [
            pl.BlockSpec((tm, SWA_KVW), lambda i, j: (i, 0)),
            pl.BlockSpec((SWA_KVW, tm), lambda i, j: (0, i)),
        ]
        out_shape += [
            jax.ShapeDtypeStruct((s, SWA_KVW), BF16),
            jax.ShapeDtypeStruct((SWA_KVW, s), BF16),
        ]
    return pl.pallas_call(
        functools.partial(_swa_proj_kernel, with_kv=kv is not None),
        grid=(s // tm, n // tn),
        in_specs=in_specs,
        out_specs=out_specs,
        out_shape=out_shape,
        scratch_shapes=scratch,
        compiler_params=_params(2),
        name="swa_proj",
    )(*args)


def _matmul_residual_kernel(a_ref, w_ref, r_ref, o_ref, *, a_transposed):
    if a_transposed:
        y = _dot_tn(a_ref[...], w_ref[...])
    else:
        y = jnp.dot(a_ref[...], w_ref[...], preferred_element_type=F32)
    o_ref[...] = r_ref[...] + y


def _matmul_residual(a, w, res, tm, a_transposed=False):
    k, n = w.shape
    s = res.shape[0]
    a_spec = (pl.BlockSpec((k, tm), lambda i: (0, i)) if a_transposed
              else pl.BlockSpec((tm, k), lambda i: (i, 0)))
    return pl.pallas_call(
        functools.partial(_matmul_residual_kernel, a_transposed=a_transposed),
        grid=(s // tm,),
        in_specs=[
            a_spec,
            pl.BlockSpec((k, n), lambda i: (0, 0)),
            pl.BlockSpec((tm, n), lambda i: (i, 0)),
        ],
        out_specs=pl.BlockSpec((tm, n), lambda i: (i, 0)),
        out_shape=jax.ShapeDtypeStruct((s, n), F32),
        compiler_params=_params(1),
        name="matmul_residual",
    )(a, w, res)


def _mlp_kernel(x_hbm, g_ref, w1_ref, w2_ref, o_ref, hn_ref, x_ref, x_sem):
    _stage_x_tile(x_hbm, x_ref, x_sem)

    def hidden_out(hn, w1, w2):
        u = jnp.maximum(jnp.dot(hn, w1, preferred_element_type=F32), 0.0)
        return jnp.dot((u * u).astype(BF16), w2, preferred_element_type=F32)

    @pl.when(pl.program_id(1) == 0)
    def _():
        w1 = w1_ref[...].astype(BF16)
        w2 = w2_ref[...].astype(BF16)
        for sl in _row_slabs(x_ref.shape[0]):
            x = x_ref[sl, :]
            hn = _rmsnorm_rows(x, g_ref[...]).astype(BF16)
            hn_ref[sl, :] = hn
            o_ref[sl, :] = x + hidden_out(hn, w1, w2)

    @pl.when(pl.program_id(1) != 0)
    def _():
        o_ref[...] += hidden_out(hn_ref[...], w1_ref[...].astype(BF16), w2_ref[...].astype(BF16))


def _mlp(x, g, w1, w2, tm, tf):
    s, d = x.shape
    f = w1.shape[1]
    return pl.pallas_call(
        _mlp_kernel,
        grid=(s // tm, f // tf),
        in_specs=[
            pl.BlockSpec(memory_space=pl.ANY),
            pl.BlockSpec((1, d), lambda i, j: (0, 0)),
            pl.BlockSpec((d, tf), lambda i, j: (0, j)),
            pl.BlockSpec((tf, d), lambda i, j: (j, 0)),
        ],
        out_specs=pl.BlockSpec((tm, d), lambda i, j: (i, 0)),
        out_shape=jax.ShapeDtypeStruct((s, d), F32),
        scratch_shapes=[pltpu.VMEM((tm, d), BF16)] + _x_stage_scratch(tm, d, f // tf),
        compiler_params=_params(2),
        name="mlp",
    )(x, g.reshape(1, d), w1, w2)


def _rounding_specs(jobs, n_steps):
    in_specs, out_specs, out_shape = [], [], []
    for w, layer, *limit in jobs:
        _, rows, cols = w.shape
        if limit:
            (rows,) = limit
        slab = rows // n_steps
        assert rows % n_steps == 0 and slab % BF16_ROWS_PER_VREG == 0
        in_specs.append(pl.BlockSpec((None, slab, cols), lambda i, layer=layer: (layer, i, 0)))
        out_specs.append(pl.BlockSpec((slab, cols), lambda i: (i, 0)))
        out_shape.append(jax.ShapeDtypeStruct((rows, cols), BF16))
    return in_specs, out_specs, out_shape


def _round_slabs(src_refs, dst_refs):
    for src, dst in zip(src_refs, dst_refs):
        dst[...] = src[...].astype(BF16)


@functools.lru_cache(maxsize=None)
def _gla_constants(c):
    i = np.arange(c)[:, None]
    t = np.arange(c)[None, :]
    mats = [t <= i, t > i]
    masks = []
    s = c // 2
    while s >= 1:
        pos = i % (2 * s)
        ref_row = i - pos + s
        upper = pos >= s
        mats.append(np.where(upper, (t > ref_row) & (t <= i), (t > i) & (t <= ref_row)))
        same_block = (i // (2 * s)) == (t // (2 * s))
        masks.append(same_block & upper & ((t % (2 * s)) < s))
        s //= 2
    masks.append(i == t)
    sums = np.concatenate(mats, axis=0).astype(np.float32)
    return sums, np.stack(masks).astype(np.float32)


def _gla_kernel(q_ref, k_ref, v_ref, r_ref, glr_ref, wg_ref, bg_ref, go_ref, sums_ref, mask_ref,
                wo_ref, res_ref, *refs, n_jobs):
    h_ref, st_ref = refs[n_jobs], refs[-1]
    _round_slabs(refs[:n_jobs], refs[n_jobs + 1:-1])
    gated = []
    c = GLA_CHUNK
    n_levels = mask_ref.shape[0] - 1

    @pl.when(pl.program_id(0) == 0)
    def _():
        st_ref[...] = jnp.zeros_like(st_ref)

    gate_in = jnp.dot(glr_ref[...].astype(BF16), wg_ref[...], preferred_element_type=F32)
    gate_in = gate_in + bg_ref[...]
    log_sig = jnp.minimum(gate_in, 0.0) - jnp.log1p(jnp.exp(-jnp.abs(gate_in)))
    log_a = log_sig * (LOG2_E / GLA_TAU)

    la_hi = log_a.astype(BF16)
    la_lo = (log_a - la_hi.astype(F32)).astype(BF16)
    chunk_sums = sums_ref[0:2 * c, :]
    part_chunk = (jnp.dot(chunk_sums, la_hi, preferred_element_type=F32)
                  + jnp.dot(chunk_sums, la_lo, preferred_element_type=F32))
    part_level = jnp.dot(sums_ref[2 * c:, :], la_hi, preferred_element_type=F32)

    for h in range(GLA_HEADS):
        qk_cols = slice(h * GLA_DK, (h + 1) * GLA_DK)
        v_cols = slice(h * GLA_DV, (h + 1) * GLA_DV)
        q = q_ref[:, qk_cols] * (GLA_DK ** -0.5)
        k = k_ref[:, qk_cols]
        v = v_ref[:, v_cols]

        b = part_chunk[0:c, qk_cols]
        state_t = st_ref[h]
        o = _dot_nt(q * jnp.exp2(b).astype(BF16), state_t.astype(BF16))

        attn = _dot_nt(q, k).astype(BF16) * mask_ref[n_levels]
        for lvl in range(n_levels):
            e = jnp.exp2(part_level[lvl * c:(lvl + 1) * c, qk_cols]).astype(BF16)
            attn = attn + _dot_nt(q * e, k * e).astype(BF16) * mask_ref[lvl]
        o = o + jnp.dot(attn, v, preferred_element_type=F32)

        k_dec = k * jnp.exp2(part_chunk[c:2 * c, qk_cols]).astype(BF16)
        st_ref[h] = state_t * jnp.exp2(b[c - 1:c, :]) + _dot_tn(v, k_dec)

        o = _rmsnorm_rows(o, go_ref[...])
        gated.append((o * r_ref[:, v_cols].astype(F32)).astype(BF16))

    h_ref[...] = res_ref[...] + jnp.dot(jnp.concatenate(gated, axis=1), wo_ref[...],
                                        preferred_element_type=F32)


def _gla_core(proj, glr, w_g2, b_g, g_o, w_o, res, rounding_jobs):
    s = proj.shape[0]
    c = GLA_CHUNK
    w_in_specs, w_out_specs, w_out_shape = _rounding_specs(rounding_jobs, s // c)
    sums, masks = _gla_constants(c)
    sums = jnp.asarray(sums, dtype=BF16)
    masks = jnp.asarray(masks, dtype=BF16)
    w_g2p = jnp.pad(w_g2, ((0, LANES - GLA_RANK), (0, 0))).astype(BF16)
    assert 2 * GLA_QK == GLA_VV
    return pl.pallas_call(
        functools.partial(_gla_kernel, n_jobs=len(rounding_jobs)),
        grid=(s // c,),
        in_specs=[
            pl.BlockSpec((c, GLA_QK), lambda i: (i, 0)),
            pl.BlockSpec((c, GLA_QK), lambda i: (i, 1)),
            pl.BlockSpec((c, GLA_VV), lambda i: (i, 1)),
            pl.BlockSpec((c, GLA_VV), lambda i: (i, 2)),
            pl.BlockSpec((c, LANES), lambda i: (i, 0)),
            pl.BlockSpec((LANES, GLA_QK), lambda i: (0, 0)),
            pl.BlockSpec((1, GLA_QK), lambda i: (0, 0)),
            pl.BlockSpec((1, GLA_DV), lambda i: (0, 0)),
            pl.BlockSpec(sums.shape, lambda i: (0, 0)),
            pl.BlockSpec(masks.shape, lambda i: (0, 0, 0)),
            pl.BlockSpec(w_o.shape, lambda i: (0, 0)),
            pl.BlockSpec((c, D_MODEL), lambda i: (i, 0)),
        ] + w_in_specs,
        out_specs=[pl.BlockSpec((c, D_MODEL), lambda i: (i, 0))] + w_out_specs,
        out_shape=[jax.ShapeDtypeStruct((s, D_MODEL), F32)] + w_out_shape,
        scratch_shapes=[pltpu.VMEM((GLA_HEADS, GLA_DV, GLA_DK), F32)],
        compiler_params=_params(1),
        name="gla_core",
    )(proj, proj, proj, proj, glr, w_g2p, b_g.reshape(1, GLA_QK), g_o.reshape(1, GLA_DV),
      sums, masks, w_o, res, *[job[0] for job in rounding_jobs])


@functools.lru_cache(maxsize=None)
def _swa_bias():
    blk = SWA_BLOCK
    kj = np.arange(2 * blk)[:, None]
    qi = np.arange(blk)[None, :]
    rel = qi + blk - kj
    valid = (rel >= 0) & (rel < SWA_WINDOW)
    first = valid & (kj >= blk)
    planes = np.stack([first, valid])
    bias = np.where(planes, 0.0, -np.inf).astype(np.float32)
    return np.tile(bias, (1, 1, SWA_GROUP))


def _swa_kernel(qt_ref, kp_ref, kc_ref, vtp_ref, vtc_ref, sink_ref, bias_ref, *refs, n_jobs):
    o_ref = refs[n_jobs]
    _round_slabs(refs[:n_jobs], refs[n_jobs + 1:])
    hd = SWA_HD
    blk = SWA_BLOCK
    k_all = jnp.concatenate([kp_ref[...], kc_ref[...]], axis=0)
    vt_all = jnp.concatenate([vtp_ref[...], vtc_ref[...]], axis=1)
    for sub in range(SWA_BLOCKS_PER_STEP):
        q_cols = slice(sub * blk, (sub + 1) * blk)
        k_band = k_all[sub * blk:(sub + 2) * blk, :]
        vt_band = vt_all[:, sub * blk:(sub + 2) * blk]
        if sub == 0:
            bias = bias_ref[jnp.minimum(pl.program_id(0), 1)]
        else:
            bias = bias_ref[1]
        for kvh in range(SWA_KVH):
            cols = [qt_ref[(kvh * SWA_GROUP + g) * hd:(kvh * SWA_GROUP + g + 1) * hd, q_cols]
                    for g in range(SWA_GROUP)]
            q_cat = jnp.concatenate(cols, axis=1)
            s_t = jnp.dot(k_band[:, kvh * hd:(kvh + 1) * hd], q_cat,
                          preferred_element_type=F32) + bias
            sink = sink_ref[kvh:kvh + 1, :] * LOG2_E
            m = jnp.maximum(jnp.max(s_t, axis=0, keepdims=True), sink)
            e = jnp.exp2(s_t - m)
            denom = jnp.sum(e, axis=0, keepdims=True) + jnp.exp2(sink - m)
            o_t = jnp.dot(vt_band[kvh * hd:(kvh + 1) * hd, :], e.astype(BF16),
                          preferred_element_type=F32)
            o_t = o_t * (1.0 / denom)
            for g in range(SWA_GROUP):
                r0 = (kvh * SWA_GROUP + g) * hd
                o_ref[r0:r0 + hd, q_cols] = o_t[:, g * blk:(g + 1) * blk].astype(o_ref.dtype)


def _swa_core(qt, k, vt, sinks, rounding_jobs):
    s = qt.shape[1]
    blk = SWA_BLOCK
    n = SWA_BLOCKS_PER_STEP
    w_in_specs, w_out_specs, w_out_shape = _rounding_specs(rounding_jobs, s // (n * blk))
    bias = jnp.asarray(_swa_bias())
    sink_rows = jnp.repeat(sinks.reshape(SWA_KVH, SWA_GROUP), blk, axis=1)

    def prev(i):
        return jnp.maximum(i * n - 1, 0)

    return pl.pallas_call(
        functools.partial(_swa_kernel, n_jobs=len(rounding_jobs)),
        grid=(s // (n * blk),),
        in_specs=[
            pl.BlockSpec((D_MODEL, n * blk), lambda i: (0, i)),
            pl.BlockSpec((blk, SWA_KVW), lambda i: (prev(i), 0)),
            pl.BlockSpec((n * blk, SWA_KVW), lambda i: (i, 0)),
            pl.BlockSpec((SWA_KVW, blk), lambda i: (0, prev(i))),
            pl.BlockSpec((SWA_KVW, n * blk), lambda i: (0, i)),
            pl.BlockSpec((SWA_KVH, SWA_GROUP * blk), lambda i: (0, 0)),
            pl.BlockSpec(bias.shape, lambda i: (0, 0, 0)),
        ] + w_in_specs,
        out_specs=[pl.BlockSpec((D_MODEL, n * blk), lambda i: (0, i))] + w_out_specs,
        out_shape=[jax.ShapeDtypeStruct((D_MODEL, s), BF16)] + w_out_shape,
        compiler_params=_params(1),
        name="swa_core",
    )(qt, k, k, vt, vt, sink_rows, bias, *[job[0] for job in rounding_jobs])


def kernel(x, norm_mix, norm_mlp, mlp_w1, mlp_w2, a_w_in, a_w_g2, a_b_g, a_g_o, a_w_o, kv_norm,
           kv_w_k, kv_w_v, kv_g_k, b_w_q, b_g_q, b_sinks, b_w_o):
    assert x.shape == (1, SEQ, D_MODEL)
    h = x[0]
    k_shared = vt_shared = None
    wq_next = None
    w_in_next = None
    a_w_in_t = jnp.swapaxes(a_w_in, 1, 2)
    for layer in range(DEPTH):
        jobs = [(mlp_w1, layer), (mlp_w2, layer)]
        if layer < N_GLA:
            if layer + 1 == N_GLA:
                jobs.append((b_w_q, 0))
            else:
                jobs.append((a_w_in_t, layer + 1, GLA_MAIN))
            proj, glr, wob = _gla_inproj(h, norm_mix[layer], a_w_in_t, a_w_o, layer, tm=TM,
                                         tn=TN if w_in_next is None else TN_WIDE,
                                         w_main=w_in_next)
            h, w1b, w2b, *rest = _gla_core(proj, glr, a_w_g2[layer], a_b_g[layer], a_g_o[layer],
                                           wob, h, jobs)
            if layer + 1 == N_GLA:
                (wq_next,) = rest
            else:
                (w_in_next,) = rest
        else:
            j = layer - N_GLA
            jobs.append((b_w_o, j))
            if layer + 1 < DEPTH:
                jobs.append((b_w_q, j + 1))
            kv = (kv_norm, kv_w_k, kv_w_v, kv_g_k) if k_shared is None else None
            qt, *shared = _swa_proj(h, norm_mix[layer], wq_next, b_g_q[j], tm=TM, tn=TN, kv=kv)
            if kv is not None:
                k_shared, vt_shared = shared
            ot, w1b, w2b, wob, *rest = _swa_core(qt, k_shared, vt_shared, b_sinks[j], jobs)
            h = _matmul_residual(ot, wob, h, tm=TM_OUT, a_transposed=True)
            if rest:
                (wq_next,) = rest
        h = _mlp(h, norm_mlp[layer], w1b, w2b, tm=TM, tf=TF)
    return h[None]
```

```python
import functools

import jax
import jax.numpy as jnp
import numpy as np
from jax import lax
from jax.experimental import pallas as pl
from jax.experimental.pallas import tpu as pltpu

F32 = jnp.float32
BF16 = jnp.bfloat16

EPS = 1e-6
LOG2_E = 1.4426950408889634
D_MODEL = 2048
SEQ = 8192
DEPTH = 4
N_GLA = DEPTH // 2
D_FF = 4 * D_MODEL

GLA_HEADS = 4
GLA_DK = 256
GLA_DV = 512
GLA_RANK = 16
GLA_TAU = 16.0
GLA_QK = GLA_HEADS * GLA_DK
GLA_VV = GLA_HEADS * GLA_DV
GLA_MAIN = 2 * GLA_QK + 2 * GLA_VV
GLA_CHUNK = 256
LANES = 128
BF16_ROWS_PER_VREG = 16
ROWS_PER_SLAB = 256

SWA_HD = 64
SWA_QH = 32
SWA_KVH = 4
SWA_GROUP = SWA_QH // SWA_KVH
SWA_WINDOW = 128
SWA_BLOCK = 128
SWA_KVW = SWA_KVH * SWA_HD
SWA_BLOCKS_PER_STEP = 4

V7X_VMEM_BYTES = 64 * 1024 * 1024
VMEM_LIMIT = V7X_VMEM_BYTES - 2 * 1024 * 1024

TM = 1024
TN = 1024
TN_WIDE = 2048
TF = 1024
WEIGHT_RING_SLOTS = 3
TM_OUT = 512


def _params(n_axes):
    return pltpu.CompilerParams(
        dimension_semantics=("arbitrary",) * n_axes, vmem_limit_bytes=VMEM_LIMIT)


def _rmsnorm_rows(x, g):
    ms = jnp.mean(x * x, axis=-1, keepdims=True)
    return x * lax.rsqrt(ms + EPS) * g


def _dot_nt(a, b):
    return lax.dot_general(a, b, (((1,), (1,)), ((), ())), preferred_element_type=F32)


def _dot_tn(a, b):
    return lax.dot_general(a, b, (((0,), (0,)), ((), ())), preferred_element_type=F32)


def _row_slabs(n_rows):
    return [slice(r, r + ROWS_PER_SLAB) for r in range(0, n_rows, ROWS_PER_SLAB)]


def _x_tile_copy(x_hbm, x_buf, sem, tile):
    rows = x_buf.shape[0]
    start = pl.multiple_of(tile * rows, rows)
    return pltpu.make_async_copy(x_hbm.at[pl.ds(start, rows), :], x_buf, sem.at[0])


def _stage_x_tile(x_hbm, x_buf, sem):
    i, j = pl.program_id(0), pl.program_id(1)

    @pl.when((i == 0) & (j == 0))
    def _():
        _x_tile_copy(x_hbm, x_buf, sem, 0).start()

    @pl.when(j == 0)
    def _():
        _x_tile_copy(x_hbm, x_buf, sem, i).wait()

    @pl.when((j == 1) & (i + 1 < pl.num_programs(0)))
    def _():
        _x_tile_copy(x_hbm, x_buf, sem, i + 1).start()


def _x_stage_scratch(tm, d, n_column_steps):
    assert n_column_steps >= 2
    return [pltpu.VMEM((tm, d), F32), pltpu.SemaphoreType.DMA((1,))]


def _dot_wt(w, hn):
    return lax.dot_general(w, hn, (((0,), (1,)), ((), ())), preferred_element_type=F32)


def _weight_ring_tile(hbm_tile, w_buf, sem):
    n_col = pl.num_programs(1)
    n_steps = pl.num_programs(0) * n_col
    step = pl.program_id(0) * n_col + pl.program_id(1)

    def copy(t):
        slot = lax.rem(t, WEIGHT_RING_SLOTS)
        return pltpu.make_async_copy(hbm_tile(lax.rem(t, n_col)), w_buf.at[slot], sem.at[slot])

    @pl.when(step == 0)
    def _():
        copy(0).start()
        copy(1).start()

    @pl.when(step + 2 < n_steps)
    def _():
        copy(step + 2).start()

    copy(step).wait()
    return w_buf.at[lax.rem(step, WEIGHT_RING_SLOTS)]


def _gla_inproj_kernel(x_hbm, g_ref, wt_ref, wgt_ref, wo_ref, o_ref, glr_ref, wob_ref, hn_ref,
                       x_ref, x_sem, *ring, first_gate_tile, ring_layer):
    _stage_x_tile(x_hbm, x_ref, x_sem)
    if ring:
        wt_hbm, tn = wt_ref, ring[0].shape[1]
        wt_ref = _weight_ring_tile(
            lambda col: wt_hbm.at[ring_layer, pl.ds(pl.multiple_of(col * tn, tn), tn), :], *ring)

    @pl.when(pl.program_id(1) == 0)
    def _():
        wob_ref[...] = wo_ref[...].astype(BF16)
        w = wt_ref[...].astype(BF16)
        wg = wgt_ref[...]
        row = lax.broadcasted_iota(jnp.int32, wg.shape, 0)
        wg = jnp.where(row < GLA_RANK, wg, 0.0).astype(BF16)
        for sl in _row_slabs(x_ref.shape[0]):
            hn = _rmsnorm_rows(x_ref[sl, :], g_ref[...]).astype(BF16)
            hn_ref[sl, :] = hn
            glr_ref[sl, :] = _dot_nt(hn, wg)
            o_ref[sl, :] = _dot_nt(hn, w).astype(o_ref.dtype)

    is_gate_tile = pl.program_id(1) >= first_gate_tile

    @pl.when((pl.program_id(1) != 0) & jnp.logical_not(is_gate_tile))
    def _():
        y = _dot_nt(hn_ref[...], wt_ref[...].astype(BF16))
        o_ref[...] = y.astype(o_ref.dtype)

    @pl.when(is_gate_tile)
    def _():
        y = _dot_nt(hn_ref[...], wt_ref[...].astype(BF16))
        o_ref[...] = (y * jax.nn.sigmoid(y)).astype(o_ref.dtype)


def _gla_inproj(x, g, wt, w_o, layer, tm, tn, w_main=None):
    s, d = x.shape
    wo_rows = w_o.shape[1] // (s // tm)
    assert w_o.shape[1] % (s // tm) == 0 and wo_rows % BF16_ROWS_PER_VREG == 0
    scratch = [pltpu.VMEM((tm, d), BF16)] + _x_stage_scratch(tm, d, GLA_MAIN // tn)
    if w_main is None:
        assert (s // tm) * (GLA_MAIN // tn) >= WEIGHT_RING_SLOTS
        main_spec = pl.BlockSpec(memory_space=pl.ANY)
        scratch += [pltpu.VMEM((WEIGHT_RING_SLOTS, tn, d), F32),
                    pltpu.SemaphoreType.DMA((WEIGHT_RING_SLOTS,))]
    else:
        main_spec = pl.BlockSpec((tn, d), lambda i, j: (j, 0))
    first_gate_tile = (2 * GLA_QK + GLA_VV) // tn
    assert first_gate_tile >= 1 and (2 * GLA_QK + GLA_VV) % tn == 0
    return pl.pallas_call(
        functools.partial(_gla_inproj_kernel, first_gate_tile=first_gate_tile, ring_layer=layer),
        grid=(s // tm, GLA_MAIN // tn),
        in_specs=[
            pl.BlockSpec(memory_space=pl.ANY),
            pl.BlockSpec((1, d), lambda i, j: (0, 0)),
            main_spec,
            pl.BlockSpec((None, LANES, d), lambda i, j: (layer, GLA_MAIN // LANES, 0)),
            pl.BlockSpec((None, wo_rows, w_o.shape[2]), lambda i, j: (layer, i, 0)),
        ],
        out_specs=[
            pl.BlockSpec((tm, tn), lambda i, j: (i, j)),
            pl.BlockSpec((tm, LANES), lambda i, j: (i, 0)),
            pl.BlockSpec((wo_rows, w_o.shape[2]), lambda i, j: (i, 0)),
        ],
        out_shape=[
            jax.ShapeDtypeStruct((s, GLA_MAIN), BF16),
            jax.ShapeDtypeStruct((s, LANES), F32),
            jax.ShapeDtypeStruct(w_o.shape[1:], BF16),
        ],
        scratch_shapes=scratch,
        compiler_params=_params(2),
        name="gla_inproj",
    )(x, g.reshape(1, d), wt if w_main is None else w_main, wt, w_o)


def _swa_proj_kernel(*refs, with_kv):
    if with_kv:
        (x_hbm, gq_ref, wq_ref, ghq_ref, gkv_ref, wk_ref, wv_ref, gk_ref,
         qt_ref, k_ref, vt_ref, hnq_ref, x_ref, x_sem, w_buf, w_sem) = refs
    else:
        x_hbm, gq_ref, wq_ref, ghq_ref, qt_ref, hnq_ref, x_ref, x_sem, w_buf, w_sem = refs
    _stage_x_tile(x_hbm, x_ref, x_sem)
    wq_hbm, tn = wq_ref, w_buf.shape[2]
    wq_ref = _weight_ring_tile(
        lambda col: wq_hbm.at[:, pl.ds(pl.multiple_of(col * tn, tn), tn)], w_buf, w_sem)

    def head_normed(q_t):
        heads = []
        for r0 in range(0, q_t.shape[0], SWA_HD):
            q_h = q_t[r0:r0 + SWA_HD, :]
            ms = jnp.mean(q_h * q_h, axis=0, keepdims=True)
            scale = lax.rsqrt(ms + EPS) * (SWA_HD ** -0.5 * LOG2_E)
            heads.append(q_h * scale * ghq_ref[...])
        return jnp.concatenate(heads, axis=0).astype(qt_ref.dtype)

    @pl.when(pl.program_id(1) == 0)
    def _():
        wq = wq_ref[...].astype(BF16)
        if with_kv:
            wk = wk_ref[...].astype(BF16)
            wv = wv_ref[...].astype(BF16)
        for sl in _row_slabs(x_ref.shape[0]):
            x = x_ref[sl, :]
            y = x * lax.rsqrt(jnp.mean(x * x, axis=-1, keepdims=True) + EPS)
            hn_q = (y * gq_ref[...]).astype(BF16)
            hnq_ref[sl, :] = hn_q
            qt_ref[:, sl] = head_normed(_dot_wt(wq, hn_q))
            if with_kv:
                hn_kv = (y * gkv_ref[...]).astype(BF16)
                k = jnp.dot(hn_kv, wk, preferred_element_type=F32)
                heads = [_rmsnorm_rows(k[:, h * SWA_HD:(h + 1) * SWA_HD], gk_ref[...])
                         for h in range(SWA_KVH)]
                k_ref[sl, :] = jnp.concatenate(heads, axis=1).astype(k_ref.dtype)
                vt_ref[:, sl] = _dot_wt(wv, hn_kv).astype(vt_ref.dtype)

    @pl.when(pl.program_id(1) != 0)
    def _():
        qt_ref[...] = head_normed(_dot_wt(wq_ref[...].astype(BF16), hnq_ref[...]))


def _swa_proj(x, g_q, w_q, g_hq, tm, tn, kv=None):
    s, d = x.shape
    n = w_q.shape[1]
    in_specs = [
        pl.BlockSpec(memory_space=pl.ANY),
        pl.BlockSpec((1, d), lambda i, j: (0, 0)),
        pl.BlockSpec(memory_space=pl.ANY),
        pl.BlockSpec((SWA_HD, 1), lambda i, j: (0, 0)),
    ]
    assert (s // tm) * (n // tn) >= WEIGHT_RING_SLOTS
    args = [x, g_q.reshape(1, d), w_q, g_hq.reshape(SWA_HD, 1)]
    out_specs = [pl.BlockSpec((tn, tm), lambda i, j: (j, i))]
    out_shape = [jax.ShapeDtypeStruct((n, s), BF16)]
    scratch = [pltpu.VMEM((tm, d), BF16)] + _x_stage_scratch(tm, d, n // tn)
    scratch += [pltpu.VMEM((WEIGHT_RING_SLOTS, d, tn), w_q.dtype),
                pltpu.SemaphoreType.DMA((WEIGHT_RING_SLOTS,))]
    if kv is not None:
        g_kv, w_k, w_v, g_k = kv
        in_specs += [
            pl.BlockSpec((1, d), lambda i, j: (0, 0)),
            pl.BlockSpec((d, SWA_KVW), lambda i, j: (0, 0)),
            pl.BlockSpec((d, SWA_KVW), lambda i, j: (0, 0)),
            pl.BlockSpec((1, SWA_HD), lambda i, j: (0, 0)),
        ]
        args += [g_kv.reshape(1, d), w_k, w_v, g_k.reshape(1, SWA_HD)]
        out_specs += [
            pl.BlockSpec((tm, SWA_KVW), lambda i, j: (i, 0)),
            pl.BlockSpec((SWA_KVW, tm), lambda i, j: (0, i)),
        ]
        out_shape += [
            jax.ShapeDtypeStruct((s, SWA_KVW), BF16),
            jax.ShapeDtypeStruct((SWA_KVW, s), BF16),
        ]
    return pl.pallas_call(
        functools.partial(_swa_proj_kernel, with_kv=kv is not None),
        grid=(s // tm, n // tn),
        in_specs=in_specs,
        out_specs=out_specs,
        out_shape=out_shape,
        scratch_shapes=scratch,
        compiler_params=_params(2),
        name="swa_proj",
    )(*args)


def _matmul_residual_kernel(a_ref, w_ref, r_ref, o_ref, *, a_transposed):
    if a_transposed:
        y = _dot_tn(a_ref[...], w_ref[...])
    else:
        y = jnp.dot(a_ref[...], w_ref[...], preferred_element_type=F32)
    o_ref[...] = r_ref[...] + y


def _matmul_residual(a, w, res, tm, a_transposed=False):
    k, n = w.shape
    s = res.shape[0]
    a_spec = (pl.BlockSpec((k, tm), lambda i: (0, i)) if a_transposed
              else pl.BlockSpec((tm, k), lambda i: (i, 0)))
    return pl.pallas_call(
        functools.partial(_matmul_residual_kernel, a_transposed=a_transposed),
        grid=(s // tm,),
        in_specs=[
            a_spec,
            pl.BlockSpec((k, n), lambda i: (0, 0)),
            pl.BlockSpec((tm, n), lambda i: (i, 0)),
        ],
        out_specs=pl.BlockSpec((tm, n), lambda i: (i, 0)),
        out_shape=jax.ShapeDtypeStruct((s, n), F32),
        compiler_params=_params(1),
        name="matmul_residual",
    )(a, w, res)


def _mlp_kernel(x_hbm, g_ref, w1_ref, w2_ref, o_ref, hn_ref, x_ref, x_sem):
    _stage_x_tile(x_hbm, x_ref, x_sem)

    def hidden_out(hn, w1, w2):
        u = jnp.maximum(jnp.dot(hn, w1, preferred_element_type=F32), 0.0)
        return jnp.dot((u * u).astype(BF16), w2, preferred_element_type=F32)

    @pl.when(pl.program_id(1) == 0)
    def _():
        w1 = w1_ref[...].astype(BF16)
        w2 = w2_ref[...].astype(BF16)
        for sl in _row_slabs(x_ref.shape[0]):
            x = x_ref[sl, :]
            hn = _rmsnorm_rows(x, g_ref[...]).astype(BF16)
            hn_ref[sl, :] = hn
            o_ref[sl, :] = x + hidden_out(hn, w1, w2)

    @pl.when(pl.program_id(1) != 0)
    def _():
        o_ref[...] += hidden_out(hn_ref[...], w1_ref[...].astype(BF16), w2_ref[...].astype(BF16))


def _mlp(x, g, w1, w2, tm, tf):
    s, d = x.shape
    f = w1.shape[1]
    return pl.pallas_call(
        _mlp_kernel,
        grid=(s // tm, f // tf),
        in_specs=[
            pl.BlockSpec(memory_space=pl.ANY),
            pl.BlockSpec((1, d), lambda i, j: (0, 0)),
            pl.BlockSpec((d, tf), lambda i, j: (0, j)),
            pl.BlockSpec((tf, d), lambda i, j: (j, 0)),
        ],
        out_specs=pl.BlockSpec((tm, d), lambda i, j: (i, 0)),
        out_shape=jax.ShapeDtypeStruct((s, d), F32),
        scratch_shapes=[pltpu.VMEM((tm, d), BF16)] + _x_stage_scratch(tm, d, f // tf),
        compiler_params=_params(2),
        name="mlp",
    )(x, g.reshape(1, d), w1, w2)


def _rounding_specs(jobs, n_steps):
    in_specs, out_specs, out_shape = [], [], []
    for w, layer, *limit in jobs:
        _, rows, cols = w.shape
        if limit:
            (rows,) = limit
        slab = rows // n_steps
        assert rows % n_steps == 0 and slab % BF16_ROWS_PER_VREG == 0
        in_specs.append(pl.BlockSpec((None, slab, cols), lambda i, layer=layer: (layer, i, 0)))
        out_specs.append(pl.BlockSpec((slab, cols), lambda i: (i, 0)))
        out_shape.append(jax.ShapeDtypeStruct((rows, cols), BF16))
    return in_specs, out_specs, out_shape


def _round_slabs(src_refs, dst_refs):
    for src, dst in zip(src_refs, dst_refs):
        dst[...] = src[...].astype(BF16)


@functools.lru_cache(maxsize=None)
def _gla_constants(c):
    i = np.arange(c)[:, None]
    t = np.arange(c)[None, :]
    mats = [t <= i, t > i]
    masks = []
    s = c // 2
    while s >= 1:
        pos = i % (2 * s)
        ref_row = i - pos + s
        upper = pos >= s
        mats.append(np.where(upper, (t > ref_row) & (t <= i), (t > i) & (t <= ref_row)))
        same_block = (i // (2 * s)) == (t // (2 * s))
        masks.append(same_block & upper & ((t % (2 * s)) < s))
        s //= 2
    masks.append(i == t)
    sums = np.concatenate(mats, axis=0).astype(np.float32)
    return sums, np.stack(masks).astype(np.float32)


def _gla_kernel(q_ref, k_ref, v_ref, r_ref, glr_ref, wg_ref, bg_ref, go_ref, sums_ref, mask_ref,
                wo_ref, res_ref, *refs, n_jobs):
    h_ref, st_ref = refs[n_jobs], refs[-1]
    _round_slabs(refs[:n_jobs], refs[n_jobs + 1:-1])
    gated = []
    c = GLA_CHUNK
    n_levels = mask_ref.shape[0] - 1

    @pl.when(pl.program_id(0) == 0)
    def _():
        st_ref[...] = jnp.zeros_like(st_ref)

    gate_in = jnp.dot(glr_ref[...].astype(BF16), wg_ref[...], preferred_element_type=F32)
    gate_in = gate_in + bg_ref[...]
    log_sig = jnp.minimum(gate_in, 0.0) - jnp.log1p(jnp.exp(-jnp.abs(gate_in)))
    log_a = log_sig * (LOG2_E / GLA_TAU)

    la_hi = log_a.astype(BF16)
    la_lo = (log_a - la_hi.astype(F32)).astype(BF16)
    chunk_sums = sums_ref[0:2 * c, :]
    part_chunk = (jnp.dot(chunk_sums, la_hi, preferred_element_type=F32)
                  + jnp.dot(chunk_sums, la_lo, preferred_element_type=F32))
    part_level = jnp.dot(sums_ref[2 * c:, :], la_hi, preferred_element_type=F32)

    for h in range(GLA_HEADS):
        qk_cols = slice(h * GLA_DK, (h + 1) * GLA_DK)
        v_cols = slice(h * GLA_DV, (h + 1) * GLA_DV)
        q = q_ref[:, qk_cols] * (GLA_DK ** -0.5)
        k = k_ref[:, qk_cols]
        v = v_ref[:, v_cols]

        b = part_chunk[0:c, qk_cols]
        state_t = st_ref[h]
        o = _dot_nt(q * jnp.exp2(b).astype(BF16), state_t.astype(BF16))

        attn = _dot_nt(q, k).astype(BF16) * mask_ref[n_levels]
        for lvl in range(n_levels):
            e = jnp.exp2(part_level[lvl * c:(lvl + 1) * c, qk_cols]).astype(BF16)
            attn = attn + _dot_nt(q * e, k * e).astype(BF16) * mask_ref[lvl]
        o = o + jnp.dot(attn, v, preferred_element_type=F32)

        k_dec = k * jnp.exp2(part_chunk[c:2 * c, qk_cols]).astype(BF16)
        st_ref[h] = state_t * jnp.exp2(b[c - 1:c, :]) + _dot_tn(v, k_dec)

        o = _rmsnorm_rows(o, go_ref[...])
        gated.append((o * r_ref[:, v_cols].astype(F32)).astype(BF16))

    h_ref[...] = res_ref[...] + jnp.dot(jnp.concatenate(gated, axis=1), wo_ref[...],
                                        preferred_element_type=F32)


def _gla_core(proj, glr, w_g2, b_g, g_o, w_o, res, rounding_jobs):
    s = proj.shape[0]
    c = GLA_CHUNK
    w_in_specs, w_out_specs, w_out_shape = _rounding_specs(rounding_jobs, s // c)
    sums, masks = _gla_constants(c)
    sums = jnp.asarray(sums, dtype=BF16)
    masks = jnp.asarray(masks, dtype=BF16)
    w_g2p = jnp.pad(w_g2, ((0, LANES - GLA_RANK), (0, 0))).astype(BF16)
    assert 2 * GLA_QK == GLA_VV
    return pl.pallas_call(
        functools.partial(_gla_kernel, n_jobs=len(rounding_jobs)),
        grid=(s // c,),
        in_specs=[
            pl.BlockSpec((c, GLA_QK), lambda i: (i, 0)),
            pl.BlockSpec((c, GLA_QK), lambda i: (i, 1)),
            pl.BlockSpec((c, GLA_VV), lambda i: (i, 1)),
            pl.BlockSpec((c, GLA_VV), lambda i: (i, 2)),
            pl.BlockSpec((c, LANES), lambda i: (i, 0)),
            pl.BlockSpec((LANES, GLA_QK), lambda i: (0, 0)),
            pl.BlockSpec((1, GLA_QK), lambda i: (0, 0)),
            pl.BlockSpec((1, GLA_DV), lambda i: (0, 0)),
            pl.BlockSpec(sums.shape, lambda i: (0, 0)),
            pl.BlockSpec(masks.shape, lambda i: (0, 0, 0)),
            pl.BlockSpec(w_o.shape, lambda i: (0, 0)),
            pl.BlockSpec((c, D_MODEL), lambda i: (i, 0)),
        ] + w_in_specs,
        out_specs=[pl.BlockSpec((c, D_MODEL), lambda i: (i, 0))] + w_out_specs,
        out_shape=[jax.ShapeDtypeStruct((s, D_MODEL), F32)] + w_out_shape,
        scratch_shapes=[pltpu.VMEM((GLA_HEADS, GLA_DV, GLA_DK), F32)],
        compiler_params=_params(1),
        name="gla_core",
    )(proj, proj, proj, proj, glr, w_g2p, b_g.reshape(1, GLA_QK), g_o.reshape(1, GLA_DV),
      sums, masks, w_o, res, *[job[0] for job in rounding_jobs])


@functools.lru_cache(maxsize=None)
def _swa_bias():
    blk = SWA_BLOCK
    kj = np.arange(2 * blk)[:, None]
    qi = np.arange(blk)[None, :]
    rel = qi + blk - kj
    valid = (rel >= 0) & (rel < SWA_WINDOW)
    first = valid & (kj >= blk)
    planes = np.stack([first, valid])
    bias = np.where(planes, 0.0, -np.inf).astype(np.float32)
    return np.tile(bias, (1, 1, SWA_GROUP))


def _swa_kernel(qt_ref, kp_ref, kc_ref, vtp_ref, vtc_ref, sink_ref, bias_ref, *refs, n_jobs):
    o_ref = refs[n_jobs]
    _round_slabs(refs[:n_jobs], refs[n_jobs + 1:])
    hd = SWA_HD
    blk = SWA_BLOCK
    k_all = jnp.concatenate([kp_ref[...], kc_ref[...]], axis=0)
    vt_all = jnp.concatenate([vtp_ref[...], vtc_ref[...]], axis=1)
    for sub in range(SWA_BLOCKS_PER_STEP):
        q_cols = slice(sub * blk, (sub + 1) * blk)
        k_band = k_all[sub * blk:(sub + 2) * blk, :]
        vt_band = vt_all[:, sub * blk:(sub + 2) * blk]
        if sub == 0:
            bias = bias_ref[jnp.minimum(pl.program_id(0), 1)]
        else:
            bias = bias_ref[1]
        for kvh in range(SWA_KVH):
            cols = [qt_ref[(kvh * SWA_GROUP + g) * hd:(kvh * SWA_GROUP + g + 1) * hd, q_cols]
                    for g in range(SWA_GROUP)]
            q_cat = jnp.concatenate(cols, axis=1)
            s_t = jnp.dot(k_band[:, kvh * hd:(kvh + 1) * hd], q_cat,
                          preferred_element_type=F32) + bias
            sink = sink_ref[kvh:kvh + 1, :] * LOG2_E
            m = jnp.maximum(jnp.max(s_t, axis=0, keepdims=True), sink)
            e = jnp.exp2(s_t - m)
            denom = jnp.sum(e, axis=0, keepdims=True) + jnp.exp2(sink - m)
            o_t = jnp.dot(vt_band[kvh * hd:(kvh + 1) * hd, :], e.astype(BF16),
                          preferred_element_type=F32)
            o_t = o_t * (1.0 / denom)
            for g in range(SWA_GROUP):
                r0 = (kvh * SWA_GROUP + g) * hd
                o_ref[r0:r0 + hd, q_cols] = o_t[:, g * blk:(g + 1) * blk].astype(o_ref.dtype)


def _swa_core(qt, k, vt, sinks, rounding_jobs):
    s = qt.shape[1]
    blk = SWA_BLOCK
    n = SWA_BLOCKS_PER_STEP
    w_in_specs, w_out_specs, w_out_shape = _rounding_specs(rounding_jobs, s // (n * blk))
    bias = jnp.asarray(_swa_bias())
    sink_rows = jnp.repeat(sinks.reshape(SWA_KVH, SWA_GROUP), blk, axis=1)

    def prev(i):
        return jnp.maximum(i * n - 1, 0)

    return pl.pallas_call(
        functools.partial(_swa_kernel, n_jobs=len(rounding_jobs)),
        grid=(s // (n * blk),),
        in_specs=[
            pl.BlockSpec((D_MODEL, n * blk), lambda i: (0, i)),
            pl.BlockSpec((blk, SWA_KVW), lambda i: (prev(i), 0)),
            pl.BlockSpec((n * blk, SWA_KVW), lambda i: (i, 0)),
            pl.BlockSpec((SWA_KVW, blk), lambda i: (0, prev(i))),
            pl.BlockSpec((SWA_KVW, n * blk), lambda i: (0, i)),
            pl.BlockSpec((SWA_KVH, SWA_GROUP * blk), lambda i: (0, 0)),
            pl.BlockSpec(bias.shape, lambda i: (0, 0, 0)),
        ] + w_in_specs,
        out_specs=[pl.BlockSpec((D_MODEL, n * blk), lambda i: (0, i))] + w_out_specs,
        out_shape=[jax.ShapeDtypeStruct((D_MODEL, s), BF16)] + w_out_shape,
        compiler_params=_params(1),
        name="swa_core",
    )(qt, k, k, vt, vt, sink_rows, bias, *[job[0] for job in rounding_jobs])


def kernel(x, norm_mix, norm_mlp, mlp_w1, mlp_w2, a_w_in, a_w_g2, a_b_g, a_g_o, a_w_o, kv_norm,
           kv_w_k, kv_w_v, kv_g_k, b_w_q, b_g_q, b_sinks, b_w_o):
    assert x.shape == (1, SEQ, D_MODEL)
    h = x[0]
    k_shared = vt_shared = None
    wq_next = None
    w_in_next = None
    a_w_in_t = jnp.swapaxes(a_w_in, 1, 2)
    for layer in range(DEPTH):
        jobs = [(mlp_w1, layer), (mlp_w2, layer)]
        if layer < N_GLA:
            if layer + 1 == N_GLA:
                jobs.append((b_w_q, 0))
            else:
                jobs.append((a_w_in_t, layer + 1, GLA_MAIN))
            proj, glr, wob = _gla_inproj(h, norm_mix[layer], a_w_in_t, a_w_o, layer, tm=TM,
                                         tn=TN if w_in_next is None else TN_WIDE,
                                         w_main=w_in_next)
            h, w1b, w2b, *rest = _gla_core(proj, glr, a_w_g2[layer], a_b_g[layer], a_g_o[layer],
                                           wob, h, jobs)
            if layer + 1 == N_GLA:
                (wq_next,) = rest
            else:
                (w_in_next,) = rest
        else:
            j = layer - N_GLA
            jobs.append((b_w_o, j))
            if layer + 1 < DEPTH:
                jobs.append((b_w_q, j + 1))
            kv = (kv_norm, kv_w_k, kv_w_v, kv_g_k) if k_shared is None else None
            qt, *shared = _swa_proj(h, norm_mix[layer], wq_next, b_g_q[j], tm=TM, tn=TN, kv=kv)
            if kv is not None:
                k_shared, vt_shared = shared
            ot, w1b, w2b, wob, *rest = _swa_core(qt, k_shared, vt_shared, b_sinks[j], jobs)
            h = _matmul_residual(ot, wob, h, tm=TM_OUT, a_transposed=True)
            if rest:
                (wq_next,) = rest
        h = _mlp(h, norm_mlp[layer], w1b, w2b, tm=TM, tf=TF)
    return h[None]
```
